```python
import math
import jax, jax.numpy as jnp
from jax import lax
import numpy as np

D_MODEL = 2048
BATCH = 2
SEQ = 8192
DEPTH = 4

N_A = DEPTH // 2
N_B = DEPTH - N_A
NORM_EPS = 1e-6

SSD_EXPAND = 2
D_INNER = SSD_EXPAND * D_MODEL
SSD_HEAD_DIM = 64
SSD_HEADS = D_INNER // SSD_HEAD_DIM
SSD_GROUPS = 8
SSD_STATE = 128
SSD_CONV = 4
SSD_CHUNK = 128
SSD_BC_DIM = SSD_GROUPS * SSD_STATE
SSD_CONV_DIM = D_INNER + 2 * SSD_BC_DIM
SSD_IN_DIM = D_INNER + SSD_CONV_DIM + SSD_HEADS
DT_MIN = 1e-3
DT_MAX = 1e-1

DIL_PATTERNS = ((128, 1), (512, 4), (2048, 16))
N_DIL = len(DIL_PATTERNS)
DIL_HEADS = 16
DIL_HEAD_DIM = D_MODEL // DIL_HEADS
DIL_WIDTH = DIL_HEADS * DIL_HEAD_DIM
DIL_BLOCK = 128
ROPE_THETA = 10000.0

MEM_LEN = 256
MEM_HEADS = 4
MEM_HEAD_DIM = 128
MEM_WIDTH = MEM_HEADS * MEM_HEAD_DIM

D_FF = -(-(8 * D_MODEL) // (3 * 256)) * 256

kernel_name = 'yoco_ssd_dilated_hybrid'


def rmsnorm(x, g):
    xf = x.astype(jnp.float32)
    y = xf * lax.rsqrt(jnp.mean(xf * xf, axis=-1, keepdims=True) + NORM_EPS)
    return (y * g.astype(jnp.float32)).astype(x.dtype)


def rope(t, positions):
    half = t.shape[-1] // 2
    inv_freq = ROPE_THETA ** (-jnp.arange(half, dtype=jnp.float32) / half)
    ang = positions.astype(jnp.float32)[:, :, None] * inv_freq
    cos = jnp.cos(ang)[:, :, None, :]
    sin = jnp.sin(ang)[:, :, None, :]
    t1 = t[..., :half].astype(jnp.float32)
    t2 = t[..., half:].astype(jnp.float32)
    return jnp.concatenate([t1 * cos - t2 * sin, t2 * cos + t1 * sin], axis=-1).astype(t.dtype)


def causal_depthwise_conv(u, w, bias):
    out = lax.conv_general_dilated(
        u, w[:, None, :].astype(u.dtype), window_strides=(1,),
        padding=[(w.shape[0] - 1, 0)], dimension_numbers=('NWC', 'WIO', 'NWC'),
        feature_group_count=u.shape[-1])
    return out + bias.astype(u.dtype)


def ssd_chunked(xh, dt, a_neg, bm, cm):
    b, s = xh.shape[0], xh.shape[1]
    c = s // SSD_CHUNK
    L = SSD_CHUNK
    G = SSD_GROUPS
    J = SSD_HEADS // G
    dtype = xh.dtype
    x = xh.reshape(b, c, L, G, J, SSD_HEAD_DIM)
    dtc = dt.reshape(b, c, L, G, J)
    bc = bm.reshape(b, c, L, G, SSD_STATE)
    cc = cm.reshape(b, c, L, G, SSD_STATE)
    a_cum = jnp.cumsum(dtc * a_neg.reshape(G, J), axis=2)
    causal = jnp.tril(jnp.ones((L, L), dtype=bool))[None, None, :, :, None, None]
    seg = a_cum[:, :, :, None] - a_cum[:, :, None, :]
    decay = jnp.exp(jnp.where(causal, seg, -jnp.inf))
    cb = jnp.einsum('bctgn,bcsgn->bctsg', cc, bc).astype(jnp.float32)
    m = cb[..., None] * decay * dtc[:, :, None]
    y_diag = jnp.einsum('bctsgj,bcsgjp->bctgjp', m.astype(dtype), x)
    w_end = jnp.exp(a_cum[:, :, -1:] - a_cum) * dtc
    states = jnp.einsum('bclgn,bclgj,bclgjp->bcgjpn', bc, w_end.astype(dtype), x)
    chunk_decay = jnp.exp(a_cum[:, :, -1])

    def step(h, inp):
        st, dec = inp
        return h * dec[..., None, None] + st, h

    h0 = jnp.zeros((b, G, J, SSD_HEAD_DIM, SSD_STATE), jnp.float32)
    _, h_in = lax.scan(step, h0, (jnp.moveaxis(states.astype(jnp.float32), 1, 0),
                                  jnp.moveaxis(chunk_decay, 1, 0)))
    h_in = jnp.moveaxis(h_in, 0, 1)
    y_off = jnp.einsum('bclgn,bcgjpn,bclgj->bclgjp', cc, h_in.astype(dtype),
                       jnp.exp(a_cum).astype(dtype))
    return (y_diag + y_off).reshape(b, s, SSD_HEADS, SSD_HEAD_DIM)


def ssd_mixer(u, w_in, conv_w, conv_b, dt_bias, a_log, d_skip, norm_w, w_out):
    b, s, _ = u.shape
    proj = u @ w_in
    z = proj[..., :D_INNER]
    xbc = proj[..., D_INNER:D_INNER + SSD_CONV_DIM]
    dt_raw = proj[..., D_INNER + SSD_CONV_DIM:]
    xbc = jax.nn.silu(causal_depthwise_conv(xbc, conv_w, conv_b))
    xs = xbc[..., :D_INNER].reshape(b, s, SSD_HEADS, SSD_HEAD_DIM)
    bm = xbc[..., D_INNER:D_INNER + SSD_BC_DIM].reshape(b, s, SSD_GROUPS, SSD_STATE)
    cm = xbc[..., D_INNER + SSD_BC_DIM:].reshape(b, s, SSD_GROUPS, SSD_STATE)
    dt = jax.nn.softplus(dt_raw.astype(jnp.float32) + dt_bias.astype(jnp.float32))
    a_neg = -jnp.exp(a_log.astype(jnp.float32))
    y = ssd_chunked(xs, dt, a_neg, bm, cm) + xs * d_skip[:, None].astype(xs.dtype)
    y = y.reshape(b, s, D_INNER) * jax.nn.silu(z)
    yg = y.reshape(b, s, SSD_GROUPS, D_INNER // SSD_GROUPS).astype(jnp.float32)
    yg = yg * lax.rsqrt(jnp.mean(yg * yg, axis=-1, keepdims=True) + NORM_EPS)
    y = (yg.reshape(b, s, D_INNER) * norm_w.astype(jnp.float32)).astype(u.dtype)
    return y @ w_out


def banded_window_attn(q, k, v, steps):
    bb, n, h, dh = q.shape
    nb = -(-n // DIL_BLOCK)
    pad = nb * DIL_BLOCK - n
    padw = ((0, 0), (0, pad), (0, 0), (0, 0))
    q = jnp.pad(q, padw)
    k = jnp.pad(k, padw)
    v = jnp.pad(v, padw)
    qb = q.reshape(bb, nb, DIL_BLOCK, h, dh)

    def with_prev(t):
        tb = t.reshape(bb, nb, DIL_BLOCK, h, dh)
        prev = jnp.concatenate([jnp.zeros_like(tb[:, :1]), tb[:, :-1]], axis=1)
        return jnp.concatenate([prev, tb], axis=2)

    kw = with_prev(k)
    vw = with_prev(v)
    sc = jnp.einsum('bnqhd,bnkhd->bnhqk', qb, kw).astype(jnp.float32) * (dh ** -0.5)
    qi = jnp.arange(DIL_BLOCK)[:, None] + DIL_BLOCK
    ki = jnp.arange(2 * DIL_BLOCK)[None, :]
    dist = qi - ki
    key_idx = jnp.arange(nb)[:, None, None] * DIL_BLOCK + ki[None] - DIL_BLOCK
    valid = (dist >= 0) & (dist <= steps) & (key_idx >= 0)
    sc = jnp.where(valid[None, :, None], sc, -jnp.inf)
    mx = jnp.max(sc, axis=-1, keepdims=True)
    p = jnp.exp(sc - mx)
    den = jnp.sum(p, axis=-1)
    o = jnp.einsum('bnhqk,bnkhd->bnqhd', p.astype(v.dtype), vw)
    o = o / jnp.swapaxes(den, 2, 3)[..., None].astype(o.dtype)
    lse = jnp.swapaxes(mx[..., 0] + jnp.log(den), 2, 3)
    o = o.reshape(bb, nb * DIL_BLOCK, h, dh)[:, :n]
    lse = lse.reshape(bb, nb * DIL_BLOCK, h)[:, :n]
    return o, lse


def dilated_group_attn(q, k, v, window, dilation):
    b, s, h, dh = q.shape
    n = s // dilation

    def to_sub(t):
        return t.reshape(b, n, dilation, h, dh).transpose(0, 2, 1, 3, 4).reshape(b * dilation, n, h, dh)

    o, lse = banded_window_attn(to_sub(q), to_sub(k), to_sub(v), window // dilation)
    o = o.reshape(b, dilation, n, h, dh).transpose(0, 2, 1, 3, 4).reshape(b, s, h, dh)
    lse = lse.reshape(b, dilation, n, h).transpose(0, 2, 1, 3).reshape(b, s, h)
    return o, lse


def dilated_mixer(u, k_all, v_all, positions, w_q, w_o):
    b, s, _ = u.shape
    q_all = rope((u @ w_q).reshape(b, s, N_DIL * DIL_HEADS, DIL_HEAD_DIM), positions)
    outs = []
    lses = []
    for g, (window, dilation) in enumerate(DIL_PATTERNS):
        sl = slice(g * DIL_HEADS, (g + 1) * DIL_HEADS)
        o, lse = dilated_group_attn(q_all[:, :, sl], k_all[:, :, sl], v_all[:, :, sl], window, dilation)
        outs.append(o)
        lses.append(lse)
    wts = jax.nn.softmax(jnp.stack(lses, axis=0), axis=0)
    o = jnp.einsum('gbsh,gbshd->bshd', wts.astype(q_all.dtype), jnp.stack(outs, axis=0))
    return o.reshape(b, s, DIL_WIDTH) @ w_o


def memory_cross_attn(u, mem_n, w_q, w_kv, w_o):
    b, s, _ = u.shape
    q = (u @ w_q).reshape(b, s, MEM_HEADS, MEM_HEAD_DIM)
    kv = mem_n @ w_kv
    k = kv[..., :MEM_WIDTH].reshape(b, MEM_LEN, MEM_HEADS, MEM_HEAD_DIM)
    v = kv[..., MEM_WIDTH:].reshape(b, MEM_LEN, MEM_HEADS, MEM_HEAD_DIM)
    sc = jnp.einsum('bshd,bmhd->bhsm', q, k).astype(jnp.float32) * (MEM_HEAD_DIM ** -0.5)
    p = jax.nn.softmax(sc, axis=-1).astype(v.dtype)
    o = jnp.einsum('bhsm,bmhd->bshd', p, v).reshape(b, s, MEM_WIDTH)
    return o @ w_o


def swiglu(u, w_in, w_out):
    gu = u @ w_in
    return (jax.nn.silu(gu[..., :D_FF]) * gu[..., D_FF:]) @ w_out


def setup_inputs(seed: int = 0) -> dict:
    key = jax.random.key(seed)
    ks = jax.random.split(key, 26)
    f32 = jnp.float32

    def dense(k, shape, fan_in):
        return jax.random.normal(k, shape, f32) * (fan_in ** -0.5)

    def gain(k, shape):
        return 1.0 + 0.02 * jax.random.normal(k, shape, f32)

    x = jax.random.normal(ks[0], (BATCH, SEQ, D_MODEL), f32)
    mem = jax.random.normal(ks[1], (BATCH, MEM_LEN, D_MODEL), f32)
    start = jax.random.randint(ks[2], (BATCH, 1), 0, 1024, dtype=jnp.int32)
    positions = start + jnp.arange(SEQ, dtype=jnp.int32)[None, :]
    norm_mix = gain(ks[3], (DEPTH, D_MODEL))
    norm_mem = gain(ks[4], (DEPTH, D_MODEL))
    norm_ffn = gain(ks[5], (DEPTH, D_MODEL))
    norm_final = gain(ks[6], (D_MODEL,))
    ssd_w_in = dense(ks[7], (N_A, D_MODEL, SSD_IN_DIM), D_MODEL)
    ssd_conv_w = dense(ks[8], (N_A, SSD_CONV, SSD_CONV_DIM), SSD_CONV)
    ssd_conv_b = 0.02 * jax.random.normal(ks[9], (N_A, SSD_CONV_DIM), f32)
    u = jax.random.uniform(ks[10], (N_A, SSD_HEADS), f32)
    dt0 = jnp.exp(u * (math.log(DT_MAX) - math.log(DT_MIN)) + math.log(DT_MIN))
    ssd_dt_bias = dt0 + jnp.log(-jnp.expm1(-dt0))
    ssd_a_log = jnp.log(jax.random.uniform(ks[11], (N_A, SSD_HEADS), f32, 1.0, 16.0))
    ssd_d = gain(ks[12], (N_A, SSD_HEADS))
    ssd_norm = gain(ks[13], (N_A, D_INNER))
    ssd_w_out = dense(ks[14], (N_A, D_INNER, D_MODEL), D_INNER)
    kv_norm = gain(ks[15], (D_MODEL,))
    w_kv_shared = dense(ks[16], (D_MODEL, 2 * N_DIL * DIL_WIDTH), D_MODEL)
    dil_w_q = dense(ks[17], (N_B, D_MODEL, N_DIL * DIL_WIDTH), D_MODEL)
    dil_w_o = dense(ks[18], (N_B, DIL_WIDTH, D_MODEL), DIL_WIDTH)
    mem_src_norm = gain(ks[19], (D_MODEL,))
    mem_w_q = dense(ks[20], (DEPTH, D_MODEL, MEM_WIDTH), D_MODEL)
    mem_w_kv = dense(ks[21], (DEPTH, D_MODEL, 2 * MEM_WIDTH), D_MODEL)
    mem_w_o = dense(ks[22], (DEPTH, MEM_WIDTH, D_MODEL), MEM_WIDTH)
    ffn_w_in = dense(ks[23], (DEPTH, D_MODEL, 2 * D_FF), D_MODEL)
    ffn_w_out = dense(ks[24], (DEPTH, D_FF, D_MODEL), D_FF)
    return {'x': x, 'mem': mem, 'positions': positions,
            'norm_mix': norm_mix, 'norm_mem': norm_mem, 'norm_ffn': norm_ffn, 'norm_final': norm_final,
            'ssd_w_in': ssd_w_in, 'ssd_conv_w': ssd_conv_w, 'ssd_conv_b': ssd_conv_b,
            'ssd_dt_bias': ssd_dt_bias, 'ssd_a_log': ssd_a_log, 'ssd_d': ssd_d,
            'ssd_norm': ssd_norm, 'ssd_w_out': ssd_w_out,
            'kv_norm': kv_norm, 'w_kv_shared': w_kv_shared, 'dil_w_q': dil_w_q, 'dil_w_o': dil_w_o,
            'mem_src_norm': mem_src_norm, 'mem_w_q': mem_w_q, 'mem_w_kv': mem_w_kv, 'mem_w_o': mem_w_o,
            'ffn_w_in': ffn_w_in, 'ffn_w_out': ffn_w_out}


def reference(x, mem, positions, norm_mix, norm_mem, norm_ffn, norm_final,
              ssd_w_in, ssd_conv_w, ssd_conv_b, ssd_dt_bias, ssd_a_log, ssd_d, ssd_norm, ssd_w_out,
              kv_norm, w_kv_shared, dil_w_q, dil_w_o,
              mem_src_norm, mem_w_q, mem_w_kv, mem_w_o, ffn_w_in, ffn_w_out):
    b, s, _ = x.shape
    mem_n = rmsnorm(mem, mem_src_norm)
    h = x
    k_sh = None
    v_sh = None
    for i in range(DEPTH):
        if i < N_A:
            h = h + ssd_mixer(rmsnorm(h, norm_mix[i]), ssd_w_in[i], ssd_conv_w[i], ssd_conv_b[i],
                              ssd_dt_bias[i], ssd_a_log[i], ssd_d[i], ssd_norm[i], ssd_w_out[i])
        else:
            if i == N_A:
                kv = rmsnorm(h, kv_norm) @ w_kv_shared
                k_sh = rope(kv[..., :N_DIL * DIL_WIDTH].reshape(b, s, N_DIL * DIL_HEADS, DIL_HEAD_DIM), positions)
                v_sh = kv[..., N_DIL * DIL_WIDTH:].reshape(b, s, N_DIL * DIL_HEADS, DIL_HEAD_DIM)
            j = i - N_A
            h = h + dilated_mixer(rmsnorm(h, norm_mix[i]), k_sh, v_sh, positions, dil_w_q[j], dil_w_o[j])
        h = h + memory_cross_attn(rmsnorm(h, norm_mem[i]), mem_n, mem_w_q[i], mem_w_kv[i], mem_w_o[i])
        h = h + swiglu(rmsnorm(h, norm_ffn[i]), ffn_w_in[i], ffn_w_out[i])
    return rmsnorm(h, norm_final)
```

```python
import functools
import math

import jax
import jax.numpy as jnp
import numpy as np
from jax import lax
from jax.experimental import pallas as pl
from jax.experimental.pallas import tpu as pltpu

F32 = jnp.float32
BF16 = jnp.bfloat16

NORM_EPS = 1e-6
SSD_HEAD_DIM = 64
SSD_GROUPS = 8
SSD_STATE = 128
SSD_CONV = 4
SSD_CHUNK = 128
DIL_PATTERNS = ((128, 1), (512, 4), (2048, 16))
DIL_HEADS = 16
DIL_HEAD_DIM = 128
DIL_BLOCK = 128
ROPE_THETA = 10000.0
MEM_HEADS = 4
MEM_HEAD_DIM = 128

LANES = 128
SUBLANES = 8
VMEM_LIMIT = 56 * 1024 * 1024


def _cparams(*sem):
    return pltpu.CompilerParams(dimension_semantics=sem, vmem_limit_bytes=VMEM_LIMIT)


def _rms(x, g):
    ms = jnp.mean(x * x, axis=-1, keepdims=True)
    return x * lax.rsqrt(ms + NORM_EPS) * g


def _silu(x):
    return x / (1.0 + jnp.exp(-x))


def _rope_table_kernel(pos_ref, freq_ref, cos_ref, sin_ref):
    ang = pos_ref[...].astype(F32) * freq_ref[...]
    lane = lax.broadcasted_iota(jnp.int32, ang.shape, 1)
    s = jnp.sin(ang)
    cos_ref[...] = jnp.cos(ang)
    sin_ref[...] = jnp.where(lane < DIL_HEAD_DIM // 2, -s, s)


def _rope_tables(positions):
    t = positions.size
    half = DIL_HEAD_DIM // 2
    inv_freq = ROPE_THETA ** (-jnp.arange(half, dtype=F32) / half)
    freq = jnp.concatenate([inv_freq, inv_freq]).reshape(1, DIL_HEAD_DIM)
    tm = 1024
    return pl.pallas_call(
        _rope_table_kernel,
        grid=(t // tm,),
        in_specs=[pl.BlockSpec((tm, 1), lambda i: (i, 0)),
                  pl.BlockSpec((1, DIL_HEAD_DIM), lambda i: (0, 0))],
        out_specs=[pl.BlockSpec((tm, DIL_HEAD_DIM), lambda i: (i, 0))] * 2,
        out_shape=[jax.ShapeDtypeStruct((t, DIL_HEAD_DIM), F32)] * 2,
        compiler_params=_cparams("parallel"),
        name="rope_tables",
    )(positions.reshape(t, 1), freq)


def _norm_matmul_kernel(x_ref, g_ref, w_ref, *rest, rope_tiles, n_tiles):
    if rope_tiles:
        cos_ref, sin_ref, o_ref, xn_ref = rest
    else:
        o_ref, xn_ref = rest
    j = pl.program_id(1)

    @pl.when(j == 0)
    def _():
        xn_ref[...] = _rms(x_ref[...], g_ref[...]).astype(BF16)

    acc = jnp.dot(xn_ref[...], w_ref[...], preferred_element_type=F32)

    def plain():
        o_ref[...] = acc.astype(o_ref.dtype)

    def roped():
        cos = cos_ref[...]
        sin = sin_ref[...]
        for hh in range(acc.shape[1] // DIL_HEAD_DIM):
            sl = slice(hh * DIL_HEAD_DIM, (hh + 1) * DIL_HEAD_DIM)
            t = acc[:, sl]
            o_ref[:, sl] = (t * cos + pltpu.roll(t, DIL_HEAD_DIM // 2, 1) * sin).astype(o_ref.dtype)

    if not rope_tiles:
        plain()
    elif rope_tiles >= n_tiles:
        roped()
    else:
        pl.when(j < rope_tiles)(roped)
        pl.when(j >= rope_tiles)(plain)


def _norm_matmul(x, g, w, out_dtype, *, tm, tn, rope=None, rope_cols=0, name="norm_matmul"):
    t, d = x.shape
    n = w.shape[1]
    n_tiles = n // tn
    rope_tiles = rope_cols // tn
    in_specs = [pl.BlockSpec((tm, d), lambda i, j: (i, 0)),
                pl.BlockSpec((1, d), lambda i, j: (0, 0)),
                pl.BlockSpec((d, tn), lambda i, j: (0, j))]
    args = [x, g.reshape(1, d), w]
    if rope_tiles:
        in_specs += [pl.BlockSpec((tm, DIL_HEAD_DIM), lambda i, j: (i, 0))] * 2
        args += list(rope)
    return pl.pallas_call(
        functools.partial(_norm_matmul_kernel, rope_tiles=rope_tiles, n_tiles=n_tiles),
        grid=(t // tm, n_tiles),
        in_specs=in_specs,
        out_specs=pl.BlockSpec((tm, tn), lambda i, j: (i, j)),
        out_shape=jax.ShapeDtypeStruct((t, n), out_dtype),
        scratch_shapes=[pltpu.VMEM((tm, d), BF16)],
        compiler_params=_cparams("parallel", "arbitrary"),
        name=name,
    )(*args)


def _matmul_residual_kernel(a_ref, w_ref, h_ref, o_ref):
    o_ref[...] = h_ref[...] + jnp.dot(a_ref[...], w_ref[...], preferred_element_type=F32)


def _matmul_residual(a, w, h, *, tm, tn, name="matmul_residual"):
    t, k = a.shape
    n = w.shape[1]
    return pl.pallas_call(
        _matmul_residual_kernel,
        grid=(n // tn, t // tm),
        in_specs=[pl.BlockSpec((tm, k), lambda j, i: (i, 0)),
                  pl.BlockSpec((k, tn), lambda j, i: (0, j)),
                  pl.BlockSpec((tm, tn), lambda j, i: (i, j))],
        out_specs=pl.BlockSpec((tm, tn), lambda j, i: (i, j)),
        out_shape=jax.ShapeDtypeStruct((t, n), F32),
        compiler_params=_cparams("parallel", "parallel"),
        name=name,
    )(a, w, h)


def _ssd_kernel(z_ref, xs_ref, b_ref, c_ref, dt_ref, convw_ref, convb_ref, dtb_ref, aneg_ref,
                dexp_ref, nw_ref, o_ref, xbuf, xc, state, *, chunks_per_seq):
    L = SSD_CHUNK
    d_inner = xs_ref.shape[1]
    bc_dim = b_ref.shape[1]
    conv_dim = d_inner + 2 * bc_dim
    gw = d_inner // SSD_GROUPS
    halo = SUBLANES
    first = pl.program_id(0) % chunks_per_seq == 0

    @pl.when(first)
    def _():
        xbuf[0:halo, :] = jnp.zeros((halo, conv_dim), F32)
        state[...] = jnp.zeros(state.shape, F32)

    @pl.when(jnp.logical_not(first))
    def _():
        xbuf[0:halo, :] = xbuf[L:L + halo, :]

    xbuf[halo:halo + L, 0:d_inner] = xs_ref[...]
    xbuf[halo:halo + L, d_inner:d_inner + bc_dim] = b_ref[...]
    xbuf[halo:halo + L, d_inner + bc_dim:conv_dim] = c_ref[...]

    slab = 512
    for s in range(conv_dim // slab):
        sl = slice(s * slab, (s + 1) * slab)
        acc = convb_ref[:, sl]
        for k in range(SSD_CONV):
            r0 = halo - (SSD_CONV - 1) + k
            acc = acc + convw_ref[k:k + 1, sl] * xbuf[r0:r0 + L, sl]
        xc[:, sl] = _silu(acc)

    raw = dt_ref[...] + dtb_ref[...]
    dt = jnp.maximum(raw, 0.0) + jnp.log1p(jnp.exp(-jnp.abs(raw)))
    acum = dt * aneg_ref[...]
    row = lax.broadcasted_iota(jnp.int32, (L, LANES), 0)
    sh = 1
    while sh < L:
        acum = acum + jnp.where(row >= sh, pltpu.roll(acum, sh, 0), 0.0)
        sh *= 2
    acum_t = acum.T
    dt_t = dt.T
    ea = jnp.exp(acum)
    wend = jnp.exp(acum[L - 1:L, :] - acum) * dt

    ti = lax.broadcasted_iota(jnp.int32, (L, L), 0)
    si = lax.broadcasted_iota(jnp.int32, (L, L), 1)
    causal = ti >= si
    lo_half = si < SSD_HEAD_DIM

    def pair_expand(v, h0):
        return jnp.where(lo_half, v[:, h0:h0 + 1], v[:, h0 + 1:h0 + 2])

    heads_per_group = gw // SSD_HEAD_DIM
    for g in range(SSD_GROUPS):
        bg = xc[:, d_inner + g * SSD_STATE:d_inner + (g + 1) * SSD_STATE]
        cg = xc[:, d_inner + bc_dim + g * SSD_STATE:d_inner + bc_dim + (g + 1) * SSD_STATE]
        cb16 = cg.astype(BF16)
        cb = lax.dot_general(cb16, bg.astype(BF16), (((1,), (1,)), ((), ())),
                             preferred_element_type=F32)
        bt16 = bg.T.astype(BF16)
        st_old = state[g]
        yoff = jnp.dot(cb16, st_old.astype(BF16), preferred_element_type=F32)
        ys, xws, cds = [], [], []
        for pp in range(heads_per_group // 2):
            h0 = g * heads_per_group + 2 * pp
            c0 = g * gw + pp * LANES
            xpair = xc[:, c0:c0 + LANES]
            xp16 = xpair.astype(BF16)
            yd = []
            for hh in (h0, h0 + 1):
                seg = acum[:, hh:hh + 1] - acum_t[hh:hh + 1, :]
                dec = jnp.exp(jnp.where(causal, seg, -jnp.inf))
                m = (cb * dec * dt_t[hh:hh + 1, :]).astype(BF16)
                yd.append(jnp.dot(m, xp16, preferred_element_type=F32))
            ea_pair = pair_expand(ea, h0)
            y = (jnp.where(lo_half, yd[0], yd[1])
                 + yoff[:, pp * LANES:(pp + 1) * LANES] * ea_pair
                 + xpair * dexp_ref[:, c0:c0 + LANES])
            y = y * _silu(z_ref[:, c0:c0 + LANES])
            ys.append(y)
            xws.append((xpair * pair_expand(wend, h0)).astype(BF16))
            cds.append(ea_pair[L - 1:L, :])
        xw = jnp.concatenate(xws, axis=1)
        cd = jnp.concatenate(cds, axis=1)
        state[g] = st_old * cd + jnp.dot(bt16, xw, preferred_element_type=F32)
        yg = jnp.concatenate(ys, axis=1)
        ms = jnp.mean(yg * yg, axis=-1, keepdims=True)
        yn = yg * lax.rsqrt(ms + NORM_EPS) * nw_ref[:, g * gw:(g + 1) * gw]
        o_ref[:, g * gw:(g + 1) * gw] = yn.astype(o_ref.dtype)


def _ssd_core(proj, dt_raw, conv_w, conv_b, dt_bias, a_log, d_skip, norm_w, *, seq, d_inner):
    t = proj.shape[0]
    L = SSD_CHUNK
    bc_dim = SSD_GROUPS * SSD_STATE
    conv_dim = d_inner + 2 * bc_dim
    heads = d_inner // SSD_HEAD_DIM
    pad = LANES - heads

    def padrow(v):
        return jnp.pad(v.astype(F32), (0, pad)).reshape(1, LANES)

    aneg = padrow(-jnp.exp(a_log.astype(F32)))
    dexp = jnp.repeat(d_skip.astype(F32), SSD_HEAD_DIM).reshape(1, d_inner)
    xs_blk = d_inner // d_inner
    b_blk = (2 * d_inner) // bc_dim
    c_blk = b_blk + 1
    const = lambda c: (0, 0)
    return pl.pallas_call(
        functools.partial(_ssd_kernel, chunks_per_seq=seq // L),
        grid=(t // L,),
        in_specs=[pl.BlockSpec((L, d_inner), lambda c: (c, 0)),
                  pl.BlockSpec((L, d_inner), lambda c: (c, xs_blk)),
                  pl.BlockSpec((L, bc_dim), lambda c: (c, b_blk)),
                  pl.BlockSpec((L, bc_dim), lambda c: (c, c_blk)),
                  pl.BlockSpec((L, LANES), lambda c: (c, 0)),
                  pl.BlockSpec((SSD_CONV, conv_dim), const),
                  pl.BlockSpec((1, conv_dim), const),
                  pl.BlockSpec((1, LANES), const),
                  pl.BlockSpec((1, LANES), const),
                  pl.BlockSpec((1, d_inner), const),
                  pl.BlockSpec((1, d_inner), const)],
        out_specs=pl.BlockSpec((L, d_inner), lambda c: (c, 0)),
        out_shape=jax.ShapeDtypeStruct((t, d_inner), BF16),
        scratch_shapes=[pltpu.VMEM((L + 2 * SUBLANES, conv_dim), F32),
                        pltpu.VMEM((L, conv_dim), F32),
                        pltpu.VMEM((SSD_GROUPS, SSD_STATE, d_inner // SSD_GROUPS), F32)],
        compiler_params=_cparams("arbitrary"),
        name="ssd_core",
    )(proj, proj, proj, proj, dt_raw, conv_w.astype(F32), conv_b.reshape(1, conv_dim).astype(F32),
      padrow(dt_bias), aneg, dexp, norm_w.reshape(1, d_inner).astype(F32))


def _dil_attn_kernel(q_ref, kp_ref, kc_ref, vp_ref, vc_ref, o_ref, lse_ref):
    blk = DIL_BLOCK
    i = pl.program_id(2)
    qi = lax.broadcasted_iota(jnp.int32, (blk, 2 * blk), 0)
    ki = lax.broadcasted_iota(jnp.int32, (blk, 2 * blk), 1)
    dist = qi + blk - ki
    first_key = jnp.where(i > 0, 0, blk)
    valid = (dist >= 0) & (dist <= blk) & (ki >= first_key)
    lane = lax.broadcasted_iota(jnp.int32, (blk, LANES), 1)
    scale = DIL_HEAD_DIM ** -0.5
    lse_tile = jnp.zeros((blk, LANES), F32)
    for h in range(DIL_HEADS):
        sl = slice(h * DIL_HEAD_DIM, (h + 1) * DIL_HEAD_DIM)
        q = q_ref[:, sl]
        k = jnp.concatenate([kp_ref[:, sl], kc_ref[:, sl]], axis=0)
        v = jnp.concatenate([vp_ref[:, sl], vc_ref[:, sl]], axis=0)
        sc = lax.dot_general(q, k, (((1,), (1,)), ((), ())), preferred_element_type=F32) * scale
        sc = jnp.where(valid, sc, -jnp.inf)
        mx = jnp.max(sc, axis=-1, keepdims=True)
        p = jnp.exp(sc - mx)
        den = jnp.sum(p, axis=-1, keepdims=True)
        o = jnp.dot(p.astype(BF16), v, preferred_element_type=F32)
        o_ref[:, sl] = o / den
        lse_tile = jnp.where(lane == h, mx + jnp.log(den), lse_tile)
    lse_ref[...] = lse_tile


def _dil_attn(q, kv, g, dilation, *, batch, seq):
    w = DIL_HEADS * DIL_HEAD_DIM
    n_groups = len(DIL_PATTERNS)
    d = dilation
    n = seq // d
    nb = n // DIL_BLOCK
    q3 = q.reshape(batch, n, d * n_groups * w)
    kv3 = kv.reshape(batch, n, d * 2 * n_groups * w)
    blk = (None, DIL_BLOCK, w)
    qcol = lambda r: r * n_groups + g
    kcol = lambda r: r * 2 * n_groups + g
    vcol = lambda r: r * 2 * n_groups + n_groups + g
    prev = lambda i: jnp.maximum(i - 1, 0)
    o, lse = pl.pallas_call(
        _dil_attn_kernel,
        grid=(batch, d, nb),
        in_specs=[pl.BlockSpec(blk, lambda b, r, i: (b, i, qcol(r))),
                  pl.BlockSpec(blk, lambda b, r, i: (b, prev(i), kcol(r))),
                  pl.BlockSpec(blk, lambda b, r, i: (b, i, kcol(r))),
                  pl.BlockSpec(blk, lambda b, r, i: (b, prev(i), vcol(r))),
                  pl.BlockSpec(blk, lambda b, r, i: (b, i, vcol(r)))],
        out_specs=[pl.BlockSpec(blk, lambda b, r, i: (b, i, r)),
                   pl.BlockSpec((None, DIL_BLOCK, LANES), lambda b, r, i: (b, i, r))],
        out_shape=[jax.ShapeDtypeStruct((batch, n, d * w), F32),
                   jax.ShapeDtypeStruct((batch, n, d * LANES), F32)],
        compiler_params=_cparams("parallel", "parallel", "arbitrary"),
        name=f"dil_attn_d{d}",
    )(q3, kv3, kv3, kv3, kv3)
    return o.reshape(batch * seq, w), lse.reshape(batch * seq, LANES)


def _dil_combine_kernel(o0_ref, o1_ref, o2_ref, l0_ref, l1_ref, l2_ref, w_ref, h_ref, out_ref):
    l0, l1, l2 = l0_ref[...], l1_ref[...], l2_ref[...]
    mx = jnp.maximum(jnp.maximum(l0, l1), l2)
    e0, e1, e2 = jnp.exp(l0 - mx), jnp.exp(l1 - mx), jnp.exp(l2 - mx)
    tot = e0 + e1 + e2
    w0, w1, w2 = e0 / tot, e1 / tot, e2 / tot
    cols = []
    for h in range(DIL_HEADS):
        sl = slice(h * DIL_HEAD_DIM, (h + 1) * DIL_HEAD_DIM)
        c = (w0[:, h:h + 1] * o0_ref[:, sl] + w1[:, h:h + 1] * o1_ref[:, sl]
             + w2[:, h:h + 1] * o2_ref[:, sl])
        cols.append(c.astype(BF16))
    a = jnp.concatenate(cols, axis=1)
    out_ref[...] = h_ref[...] + jnp.dot(a, w_ref[...], preferred_element_type=F32)


def _dil_combine(os_, lses, w_o, h, *, tm):
    t, w = os_[0].shape
    d = w_o.shape[1]
    row = lambda i: (i, 0)
    return pl.pallas_call(
        _dil_combine_kernel,
        grid=(t // tm,),
        in_specs=[pl.BlockSpec((tm, w), row)] * 3 + [pl.BlockSpec((tm, LANES), row)] * 3
                 + [pl.BlockSpec((w, d), lambda i: (0, 0)), pl.BlockSpec((tm, d), row)],
        out_specs=pl.BlockSpec((tm, d), row),
        out_shape=jax.ShapeDtypeStruct((t, d), F32),
        compiler_params=_cparams("parallel"),
        name="dil_combine",
    )(*os_, *lses, w_o, h)


def _mem_kv_kernel(mem_ref, g_ref, w_ref, o_ref):
    xn = _rms(mem_ref[...], g_ref[...]).astype(BF16)
    o_ref[...] = jnp.dot(xn, w_ref[...], preferred_element_type=F32).astype(o_ref.dtype)


def _mem_kv(mem, g, w_kv):
    b, m, d = mem.shape
    depth, _, n = w_kv.shape
    return pl.pallas_call(
        _mem_kv_kernel,
        grid=(depth, b),
        in_specs=[pl.BlockSpec((None, m, d), lambda l, bb: (bb, 0, 0)),
                  pl.BlockSpec((1, d), lambda l, bb: (0, 0)),
                  pl.BlockSpec((None, d, n), lambda l, bb: (l, 0, 0))],
        out_specs=pl.BlockSpec((None, None, m, n), lambda l, bb: (l, bb, 0, 0)),
        out_shape=jax.ShapeDtypeStruct((depth, b, m, n), BF16),
        compiler_params=_cparams("parallel", "parallel"),
        name="mem_kv",
    )(mem, g.reshape(1, d), w_kv)


def _mem_attn_kernel(h_ref, g_ref, wq_ref, kv_ref, wo_ref, o_ref):
    x = h_ref[...]
    xn = _rms(x, g_ref[...]).astype(BF16)
    q = jnp.dot(xn, wq_ref[...], preferred_element_type=F32)
    width = MEM_HEADS * MEM_HEAD_DIM
    scale = MEM_HEAD_DIM ** -0.5
    outs = []
    for hh in range(MEM_HEADS):
        sl = slice(hh * MEM_HEAD_DIM, (hh + 1) * MEM_HEAD_DIM)
        k = kv_ref[:, sl]
        v = kv_ref[:, width + hh * MEM_HEAD_DIM:width + (hh + 1) * MEM_HEAD_DIM]
        sc = lax.dot_general(q[:, sl].astype(BF16), k, (((1,), (1,)), ((), ())),
                             preferred_element_type=F32) * scale
        mx = jnp.max(sc, axis=-1, keepdims=True)
        p = jnp.exp(sc - mx)
        p = p / jnp.sum(p, axis=-1, keepdims=True)
        outs.append(jnp.dot(p.astype(BF16), v, preferred_element_type=F32).astype(BF16))
    a = jnp.concatenate(outs, axis=1)
    o_ref[...] = x + jnp.dot(a, wo_ref[...], preferred_element_type=F32)


def _mem_attn(h, g, w_q, kv, w_o, layer, *, batch, seq, tm):
    t, d = h.shape
    width = MEM_HEADS * MEM_HEAD_DIM
    m = kv.shape[2]
    per_b = seq // tm
    return pl.pallas_call(
        _mem_attn_kernel,
        grid=(t // tm,),
        in_specs=[pl.BlockSpec((tm, d), lambda i: (i, 0)),
                  pl.BlockSpec((1, d), lambda i: (0, 0)),
                  pl.BlockSpec((None, d, width), lambda i: (layer, 0, 0)),
                  pl.BlockSpec((None, None, m, 2 * width), lambda i: (layer, i // per_b, 0, 0)),
                  pl.BlockSpec((None, width, d), lambda i: (layer, 0, 0))],
        out_specs=pl.BlockSpec((tm, d), lambda i: (i, 0)),
        out_shape=jax.ShapeDtypeStruct((t, d), F32),
        compiler_params=_cparams("parallel"),
        name="mem_attn",
    )(h, g.reshape(1, d), w_q, kv, w_o)


def _ffn_kernel(h_ref, g_ref, wg_ref, wu_ref, wo_ref, gf_ref, o_ref, xn_ref, *, final_norm):
    f = pl.program_id(1)

    @pl.when(f == 0)
    def _():
        x = h_ref[...]
        xn_ref[...] = _rms(x, g_ref[...]).astype(BF16)
        o_ref[...] = x

    xn = xn_ref[...]
    gate = jnp.dot(xn, wg_ref[...], preferred_element_type=F32)
    up = jnp.dot(xn, wu_ref[...], preferred_element_type=F32)
    a = (_silu(gate) * up).astype(BF16)
    o_ref[...] += jnp.dot(a, wo_ref[...], preferred_element_type=F32)

    if final_norm:
        @pl.when(f == pl.num_programs(1) - 1)
        def _():
            o_ref[...] = _rms(o_ref[...], gf_ref[...])


def _ffn(h, g, w_in, w_out, layer, g_final, *, tm, tf, final_norm):
    t, d = h.shape
    d_ff = w_out.shape[1]
    nf = d_ff // tf
    return pl.pallas_call(
        functools.partial(_ffn_kernel, final_norm=final_norm),
        grid=(t // tm, nf),
        in_specs=[pl.BlockSpec((tm, d), lambda i, f: (i, 0)),
                  pl.BlockSpec((1, d), lambda i, f: (0, 0)),
                  pl.BlockSpec((None, d, tf), lambda i, f: (layer, 0, f)),
                  pl.BlockSpec((None, d, tf), lambda i, f: (layer, 0, nf + f)),
                  pl.BlockSpec((None, tf, d), lambda i, f: (layer, f, 0)),
                  pl.BlockSpec((1, d), lambda i, f: (0, 0))],
        out_specs=pl.BlockSpec((tm, d), lambda i, f: (i, 0)),
        out_shape=jax.ShapeDtypeStruct((t, d), F32),
        scratch_shapes=[pltpu.VMEM((tm, d), BF16)],
        compiler_params=_cparams("parallel", "arbitrary"),
        name="ffn",
    )(h, g.reshape(1, d), w_in, w_in, w_out, g_final.reshape(1, d))


def kernel(x, mem, positions, norm_mix, norm_mem, norm_ffn, norm_final, ssd_w_in, ssd_conv_w, ssd_conv_b, ssd_dt_bias, ssd_a_log, ssd_d, ssd_norm, ssd_w_out, kv_norm, w_kv_shared, dil_w_q, dil_w_o, mem_src_norm, mem_w_q, mem_w_kv, mem_w_o, ffn_w_in, ffn_w_out):
    batch, seq, d_model = x.shape
    depth = norm_mix.shape[0]
    n_a = ssd_w_in.shape[0]
    d_inner = ssd_w_out.shape[1]
    heads = d_inner // SSD_HEAD_DIM
    zx_dim = ssd_w_in.shape[2] - heads
    dil_w = DIL_HEADS * DIL_HEAD_DIM
    n_dil = len(DIL_PATTERNS)
    t = batch * seq

    h = x.reshape(t, d_model)
    rope = _rope_tables(positions)
    mem_kv = _mem_kv(mem, mem_src_norm, mem_w_kv.astype(BF16))
    mem_w_q16 = mem_w_q.astype(BF16)
    mem_w_o16 = mem_w_o.astype(BF16)
    ffn_w_in16 = ffn_w_in.astype(BF16)
    ffn_w_out16 = ffn_w_out.astype(BF16)

    kv_sh = None
    for i in range(depth):
        if i < n_a:
            w_zx = ssd_w_in[i, :, :zx_dim].astype(BF16)
            w_dt = jnp.pad(ssd_w_in[i, :, zx_dim:], ((0, 0), (0, LANES - heads))).astype(BF16)
            proj = _norm_matmul(h, norm_mix[i], w_zx, F32, tm=1024, tn=512, name="ssd_in_proj")
            dt_raw = _norm_matmul(h, norm_mix[i], w_dt, F32, tm=1024, tn=LANES, name="ssd_dt_proj")
            y = _ssd_core(proj, dt_raw, ssd_conv_w[i], ssd_conv_b[i], ssd_dt_bias[i], ssd_a_log[i],
                          ssd_d[i], ssd_norm[i], seq=seq, d_inner=d_inner)
            h = _matmul_residual(y, ssd_w_out[i].astype(BF16), h, tm=512, tn=1024, name="ssd_out_proj")
        else:
            if kv_sh is None:
                kv_sh = _norm_matmul(h, kv_norm, w_kv_shared.astype(BF16), BF16, tm=1024, tn=512,
                                     rope=rope, rope_cols=n_dil * dil_w, name="shared_kv_proj")
            j = i - n_a
            q = _norm_matmul(h, norm_mix[i], dil_w_q[j].astype(BF16), BF16, tm=1024, tn=512,
                             rope=rope, rope_cols=n_dil * dil_w, name="dil_q_proj")
            outs, lses = [], []
            for g, (window, dilation) in enumerate(DIL_PATTERNS):
                assert window // dilation == DIL_BLOCK
                o, lse = _dil_attn(q, kv_sh, g, dilation, batch=batch, seq=seq)
                outs.append(o)
                lses.append(lse)
            h = _dil_combine(outs, lses, dil_w_o[j].astype(BF16), h, tm=256)
        h = _mem_attn(h, norm_mem[i], mem_w_q16, mem_kv, mem_w_o16, i, batch=batch, seq=seq, tm=512)
        h = _ffn(h, norm_ffn[i], ffn_w_in16, ffn_w_out16, i, norm_final, tm=512, tf=512,
                 final_norm=(i == depth - 1))
    return h.reshape(batch, seq, d_model)
```

```python
import functools
import math

import jax
import jax.numpy as jnp
import numpy as np
from jax import lax
from jax.experimental import pallas as pl
from jax.experimental.pallas import tpu as pltpu

F32 = jnp.float32
BF16 = jnp.bfloat16

NORM_EPS = 1e-6
SSD_HEAD_DIM = 64
SSD_GROUPS = 8
SSD_STATE = 128
SSD_CONV = 4
SSD_CHUNK = 128
DIL_PATTERNS = ((128, 1), (512, 4), (2048, 16))
DIL_HEADS = 16
DIL_HEAD_DIM = 128
DIL_BLOCK = 128
ROPE_THETA = 10000.0
MEM_HEADS = 4
MEM_HEAD_DIM = 128

LANES = 128
SUBLANES = 8
VMEM_LIMIT = 56 * 1024 * 1024


def _cparams(*sem):
    return pltpu.CompilerParams(dimension_semantics=sem, vmem_limit_bytes=VMEM_LIMIT)


def _rms(x, g):
    ms = jnp.mean(x * x, axis=-1, keepdims=True)
    return x * lax.rsqrt(ms + NORM_EPS) * g


def _silu(x):
    return x / (1.0 + jnp.exp(-x))


def _rope_table_kernel(pos_ref, freq_ref, cos_ref, sin_ref):
    ang = pos_ref[...].astype(F32) * freq_ref[...]
    lane = lax.broadcasted_iota(jnp.int32, ang.shape, 1)
    s = jnp.sin(ang)
    cos_ref[...] = jnp.cos(ang)
    sin_ref[...] = jnp.where(lane < DIL_HEAD_DIM // 2, -s, s)


def _rope_tables(positions):
    t = positions.size
    half = DIL_HEAD_DIM // 2
    inv_freq = ROPE_THETA ** (-jnp.arange(half, dtype=F32) / half)
    freq = jnp.concatenate([inv_freq, inv_freq]).reshape(1, DIL_HEAD_DIM)
    tm = 1024
    return pl.pallas_call(
        _rope_table_kernel,
        grid=(t // tm,),
        in_specs=[pl.BlockSpec((tm, 1), lambda i: (i, 0)),
                  pl.BlockSpec((1, DIL_HEAD_DIM), lambda i: (0, 0))],
        out_specs=[pl.BlockSpec((tm, DIL_HEAD_DIM), lambda i: (i, 0))] * 2,
        out_shape=[jax.ShapeDtypeStruct((t, DIL_HEAD_DIM), F32)] * 2,
        compiler_params=_cparams("parallel"),
        name="rope_tables",
    )(positions.reshape(t, 1), freq)


def _norm_matmul_kernel(x_ref, g_ref, w_ref, o_ref, xn_ref):
    @pl.when(pl.program_id(1) == 0)
    def _():
        xn_ref[...] = _rms(x_ref[...], g_ref[...]).astype(BF16)

    o_ref[...] = jnp.dot(xn_ref[...], w_ref[...], preferred_element_type=F32).astype(o_ref.dtype)


def _norm_matmul(x, g, w, out_dtype, *, tm, tn, name="norm_matmul"):
    t, d = x.shape
    n = w.shape[1]
    return pl.pallas_call(
        _norm_matmul_kernel,
        grid=(t // tm, n // tn),
        in_specs=[pl.BlockSpec((tm, d), lambda i, j: (i, 0)),
                  pl.BlockSpec((1, d), lambda i, j: (0, 0)),
                  pl.BlockSpec((d, tn), lambda i, j: (0, j))],
        out_specs=pl.BlockSpec((tm, tn), lambda i, j: (i, j)),
        out_shape=jax.ShapeDtypeStruct((t, n), out_dtype),
        scratch_shapes=[pltpu.VMEM((tm, d), BF16)],
        compiler_params=_cparams("parallel", "arbitrary"),
        name=name,
    )(x, g.reshape(1, d), w)


def _norm_matmul_dil_kernel(x_ref, g_ref, w_ref, cos_ref, sin_ref, *rest, dilations, rope_flags, tiles_per_out):
    n_out = len(dilations)
    o_refs = rest[:n_out]
    xn_ref, acc_ref = rest[n_out:]
    j = pl.program_id(1)

    @pl.when(j == 0)
    def _():
        xn_ref[...] = _rms(x_ref[...], g_ref[...]).astype(BF16)

    acc = jnp.dot(xn_ref[...], w_ref[...], preferred_element_type=F32)
    tm, tn = acc.shape
    for a in range(n_out):
        @pl.when((j >= a * tiles_per_out) & (j < (a + 1) * tiles_per_out))
        def _(a=a):
            d = dilations[a]
            for hh in range(tn // DIL_HEAD_DIM):
                sl = slice(hh * DIL_HEAD_DIM, (hh + 1) * DIL_HEAD_DIM)
                t = acc[:, sl]
                if rope_flags[a]:
                    t = t * cos_ref[...] + pltpu.roll(t, DIL_HEAD_DIM // 2, 1) * sin_ref[...]
                acc_ref[hh] = t
                for r in range(d):
                    o_refs[a][r, :, sl] = acc_ref[hh, pl.ds(r, tm // d, stride=d), :].astype(BF16)


def _norm_matmul_dil(x, g, w, rope, *, dilations, rope_flags, batch, seq, tm, tn, name):
    t, d_model = x.shape
    width = DIL_HEADS * DIL_HEAD_DIM
    n_out = len(dilations)
    tiles_per_out = width // tn
    per_b = seq // tm

    def out_spec(a):
        d = dilations[a]
        return pl.BlockSpec(
            (None, d, tm // d, tn),
            lambda i, j: (i // per_b, 0, i % per_b, jnp.clip(j - a * tiles_per_out, 0, tiles_per_out - 1)))

    return pl.pallas_call(
        functools.partial(_norm_matmul_dil_kernel, dilations=dilations, rope_flags=rope_flags,
                          tiles_per_out=tiles_per_out),
        grid=(t // tm, n_out * tiles_per_out),
        in_specs=[pl.BlockSpec((tm, d_model), lambda i, j: (i, 0)),
                  pl.BlockSpec((1, d_model), lambda i, j: (0, 0)),
                  pl.BlockSpec((d_model, tn), lambda i, j: (0, j)),
                  pl.BlockSpec((tm, DIL_HEAD_DIM), lambda i, j: (i, 0)),
                  pl.BlockSpec((tm, DIL_HEAD_DIM), lambda i, j: (i, 0))],
        out_specs=[out_spec(a) for a in range(n_out)],
        out_shape=[jax.ShapeDtypeStruct((batch, dd, seq // dd, width), BF16) for dd in dilations],
        scratch_shapes=[pltpu.VMEM((tm, d_model), BF16), pltpu.VMEM((tn // DIL_HEAD_DIM, tm, DIL_HEAD_DIM), F32)],
        compiler_params=_cparams("arbitrary", "arbitrary"),
        name=name,
    )(x, g.reshape(1, d_model), w, *rope)


def _matmul_residual_kernel(a_ref, w_ref, h_ref, o_ref):
    o_ref[...] = h_ref[...] + jnp.dot(a_ref[...], w_ref[...], preferred_element_type=F32)


def _matmul_residual(a, w, h, *, tm, tn, name="matmul_residual"):
    t, k = a.shape
    n = w.shape[1]
    return pl.pallas_call(
        _matmul_residual_kernel,
        grid=(n // tn, t // tm),
        in_specs=[pl.BlockSpec((tm, k), lambda j, i: (i, 0)),
                  pl.BlockSpec((k, tn), lambda j, i: (0, j)),
                  pl.BlockSpec((tm, tn), lambda j, i: (i, j))],
        out_specs=pl.BlockSpec((tm, tn), lambda j, i: (i, j)),
        out_shape=jax.ShapeDtypeStruct((t, n), F32),
        compiler_params=_cparams("parallel", "parallel"),
        name=name,
    )(a, w, h)


def _ssd_kernel(z_ref, xs_ref, b_ref, c_ref, dt_ref, convw_ref, convb_ref, dtb_ref, aneg_ref,
                dexp_ref, nw_ref, o_ref, xbuf, xc, state, *, chunks_per_seq):
    L = SSD_CHUNK
    d_inner = xs_ref.shape[1]
    bc_dim = b_ref.shape[1]
    conv_dim = d_inner + 2 * bc_dim
    gw = d_inner // SSD_GROUPS
    halo = SUBLANES
    first = pl.program_id(0) % chunks_per_seq == 0

    @pl.when(first)
    def _():
        xbuf[:, 0:halo, :] = jnp.zeros((xbuf.shape[0], halo, LANES), F32)
        state[...] = jnp.zeros(state.shape, F32)

    @pl.when(jnp.logical_not(first))
    def _():
        xbuf[:, 0:halo, :] = xbuf[:, L:L + halo, :]

    n_cb = conv_dim // LANES
    for cb in range(n_cb):
        c0 = cb * LANES
        if c0 < d_inner:
            src = xs_ref[:, c0:c0 + LANES]
        elif c0 < d_inner + bc_dim:
            src = b_ref[:, c0 - d_inner:c0 - d_inner + LANES]
        else:
            src = c_ref[:, c0 - d_inner - bc_dim:c0 - d_inner - bc_dim + LANES]
        xbuf[cb, halo:halo + L, :] = src

    nres = SUBLANES
    rows = L // nres
    for cb in range(n_cb):
        sl = slice(cb * LANES, (cb + 1) * LANES)
        taps = [convw_ref[k:k + 1, sl] for k in range(SSD_CONV)]
        bias = convb_ref[:, sl]
        cur = [xbuf[cb, pl.ds(halo + r, rows, stride=nres), :] for r in range(nres)]
        prv = {r: xbuf[cb, pl.ds(r, rows, stride=nres), :]
               for r in range(nres - SSD_CONV + 1, nres)}
        for r in range(nres):
            acc = bias
            for k in range(SSD_CONV):
                q = r - (SSD_CONV - 1) + k
                acc = acc + taps[k] * (cur[q] if q >= 0 else prv[q + nres])
            xc[cb, pl.ds(r, rows, stride=nres), :] = _silu(acc)

    raw = dt_ref[...] + dtb_ref[...]
    dt = jnp.maximum(raw, 0.0) + jnp.log1p(jnp.exp(-jnp.abs(raw)))
    acum = dt * aneg_ref[...]
    row = lax.broadcasted_iota(jnp.int32, (L, LANES), 0)
    sh = 1
    while sh < L:
        acum = acum + jnp.where(row >= sh, pltpu.roll(acum, sh, 0), 0.0)
        sh *= 2
    acum_t = acum.T
    dt_t = dt.T
    ea = jnp.exp(acum)
    wend = jnp.exp(acum[L - 1:L, :] - acum) * dt

    ti = lax.broadcasted_iota(jnp.int32, (L, L), 0)
    si = lax.broadcasted_iota(jnp.int32, (L, L), 1)
    causal = ti >= si
    lo_half = si < SSD_HEAD_DIM

    def pair_expand(v, h0):
        return jnp.where(lo_half, v[:, h0:h0 + 1], v[:, h0 + 1:h0 + 2])

    heads_per_group = gw // SSD_HEAD_DIM
    for g in range(SSD_GROUPS):
        bg = xc[(d_inner + g * SSD_STATE) // LANES]
        cg = xc[(d_inner + bc_dim + g * SSD_STATE) // LANES]
        cb16 = cg.astype(BF16)
        cb = lax.dot_general(cb16, bg.astype(BF16), (((1,), (1,)), ((), ())),
                             preferred_element_type=F32)
        bt16 = bg.T.astype(BF16)
        st_old = state[g]
        yoff = jnp.dot(cb16, st_old.astype(BF16), preferred_element_type=F32)
        ys, xws, cds = [], [], []
        for pp in range(heads_per_group // 2):
            h0 = g * heads_per_group + 2 * pp
            c0 = g * gw + pp * LANES
            xpair = xc[c0 // LANES]
            xp16 = xpair.astype(BF16)
            yd = []
            for hh in (h0, h0 + 1):
                seg = acum[:, hh:hh + 1] - acum_t[hh:hh + 1, :]
                dec = jnp.exp(jnp.where(causal, seg, -jnp.inf))
                m = (cb * dec * dt_t[hh:hh + 1, :]).astype(BF16)
                yd.append(jnp.dot(m, xp16, preferred_element_type=F32))
            ea_pair = pair_expand(ea, h0)
            y = (jnp.where(lo_half, yd[0], yd[1])
                 + yoff[:, pp * LANES:(pp + 1) * LANES] * ea_pair
                 + xpair * dexp_ref[:, c0:c0 + LANES])
            y = y * _silu(z_ref[:, c0:c0 + LANES])
            ys.append(y)
            xws.append((xpair * pair_expand(wend, h0)).astype(BF16))
            cds.append(ea_pair[L - 1:L, :])
        xw = jnp.concatenate(xws, axis=1)
        cd = jnp.concatenate(cds, axis=1)
        state[g] = st_old * cd + jnp.dot(bt16, xw, preferred_element_type=F32)
        yg = jnp.concatenate(ys, axis=1)
        ms = jnp.mean(yg * yg, axis=-1, keepdims=True)
        yn = yg * lax.rsqrt(ms + NORM_EPS) * nw_ref[:, g * gw:(g + 1) * gw]
        o_ref[:, g * gw:(g + 1) * gw] = yn.astype(o_ref.dtype)


def _ssd_core(proj, dt_raw, conv_w, conv_b, dt_bias, a_log, d_skip, norm_w, *, seq, d_inner):
    t = proj.shape[0]
    L = SSD_CHUNK
    bc_dim = SSD_GROUPS * SSD_STATE
    conv_dim = d_inner + 2 * bc_dim
    heads = d_inner // SSD_HEAD_DIM
    pad = LANES - heads

    def padrow(v):
        return jnp.pad(v.astype(F32), (0, pad)).reshape(1, LANES)

    aneg = padrow(-jnp.exp(a_log.astype(F32)))
    dexp = jnp.repeat(d_skip.astype(F32), SSD_HEAD_DIM).reshape(1, d_inner)
    xs_blk = d_inner // d_inner
    b_blk = (2 * d_inner) // bc_dim
    c_blk = b_blk + 1
    const = lambda c: (0, 0)
    return pl.pallas_call(
        functools.partial(_ssd_kernel, chunks_per_seq=seq // L),
        grid=(t // L,),
        in_specs=[pl.BlockSpec((L, d_inner), lambda c: (c, 0)),
                  pl.BlockSpec((L, d_inner), lambda c: (c, xs_blk)),
                  pl.BlockSpec((L, bc_dim), lambda c: (c, b_blk)),
                  pl.BlockSpec((L, bc_dim), lambda c: (c, c_blk)),
                  pl.BlockSpec((L, LANES), lambda c: (c, 0)),
                  pl.BlockSpec((SSD_CONV, conv_dim), const),
                  pl.BlockSpec((1, conv_dim), const),
                  pl.BlockSpec((1, LANES), const),
                  pl.BlockSpec((1, LANES), const),
                  pl.BlockSpec((1, d_inner), const),
                  pl.BlockSpec((1, d_inner), const)],
        out_specs=pl.BlockSpec((L, d_inner), lambda c: (c, 0)),
        out_shape=jax.ShapeDtypeStruct((t, d_inner), BF16),
        scratch_shapes=[pltpu.VMEM((conv_dim // LANES, L + 2 * SUBLANES, LANES), F32),
                        pltpu.VMEM((conv_dim // LANES, L, LANES), F32),
                        pltpu.VMEM((SSD_GROUPS, SSD_STATE, d_inner // SSD_GROUPS), F32)],
        compiler_params=_cparams("arbitrary"),
        name="ssd_core",
    )(proj, proj, proj, proj, dt_raw, conv_w.astype(F32), conv_b.reshape(1, conv_dim).astype(F32),
      padrow(dt_bias), aneg, dexp, norm_w.reshape(1, d_inner).astype(F32))


def _dil_attn_kernel(q_ref, k_ref, v_ref, o_ref, lse_ref, kprev, vprev):
    blk = DIL_BLOCK
    i = pl.program_id(1)

    @pl.when(i == 0)
    def _():
        kprev[...] = jnp.zeros(kprev.shape, kprev.dtype)
        vprev[...] = jnp.zeros(vprev.shape, vprev.dtype)

    qi = lax.broadcasted_iota(jnp.int32, (blk, 2 * blk), 0)
    ki = lax.broadcasted_iota(jnp.int32, (blk, 2 * blk), 1)
    dist = qi + blk - ki
    first_key = jnp.where(i > 0, 0, blk)
    valid = (dist >= 0) & (dist <= blk) & (ki >= first_key)
    lane = lax.broadcasted_iota(jnp.int32, (blk, LANES), 1)
    scale = DIL_HEAD_DIM ** -0.5
    lse_tile = jnp.zeros((blk, LANES), F32)
    for h in range(DIL_HEADS):
        sl = slice(h * DIL_HEAD_DIM, (h + 1) * DIL_HEAD_DIM)
        k = jnp.concatenate([kprev[:, sl], k_ref[:, sl]], axis=0)
        v = jnp.concatenate([vprev[:, sl], v_ref[:, sl]], axis=0)
        sc = lax.dot_general(q_ref[:, sl], k, (((1,), (1,)), ((), ())), preferred_element_type=F32) * scale
        sc = jnp.where(valid, sc, -jnp.inf)
        mx = jnp.max(sc, axis=-1, keepdims=True)
        p = jnp.exp(sc - mx)
        den = jnp.sum(p, axis=-1, keepdims=True)
        o = jnp.dot(p.astype(BF16), v, preferred_element_type=F32)
        o_ref[:, sl] = o / den
        lse_tile = jnp.where(lane == h, mx + jnp.log(den), lse_tile)
    lse_ref[...] = lse_tile
    kprev[...] = k_ref[...]
    vprev[...] = v_ref[...]


def _dil_attn(q, k, v):
    b, d, n, w = q.shape
    seqs = b * d
    blk = pl.BlockSpec((None, DIL_BLOCK, w), lambda s, i: (s, i, 0))
    o, lse = pl.pallas_call(
        _dil_attn_kernel,
        grid=(seqs, n // DIL_BLOCK),
        in_specs=[blk, blk, blk],
        out_specs=[blk, pl.BlockSpec((None, DIL_BLOCK, LANES), lambda s, i: (s, i, 0))],
        out_shape=[jax.ShapeDtypeStruct((seqs, n, w), F32),
                   jax.ShapeDtypeStruct((seqs, n, LANES), F32)],
        scratch_shapes=[pltpu.VMEM((DIL_BLOCK, w), BF16), pltpu.VMEM((DIL_BLOCK, w), BF16)],
        compiler_params=_cparams("arbitrary", "arbitrary"),
        name=f"dil_attn_d{d}",
    )(q.reshape(seqs, n, w), k.reshape(seqs, n, w), v.reshape(seqs, n, w))
    return o.reshape(b, d, n, w), lse.reshape(b, d, n, LANES)


def _dil_combine_kernel(*refs, dilations):
    n_g = len(dilations)
    o_refs, l_refs = refs[:n_g], refs[n_g:2 * n_g]
    w_ref, h_ref, out_ref, onat, lnat = refs[2 * n_g:]
    tm = h_ref.shape[0]

    def head_slice(h):
        return slice(h * DIL_HEAD_DIM, (h + 1) * DIL_HEAD_DIM)

    o_heads, ls = [], []
    slot = 0
    for gi, d in enumerate(dilations):
        if d == 1:
            o_heads.append(lambda h, gi=gi: o_refs[gi][0, :, head_slice(h)])
            ls.append(l_refs[gi][0])
            continue
        for r in range(d):
            rows = pl.ds(r, tm // d, stride=d)
            lnat[slot, rows, :] = l_refs[gi][r]
            for h in range(DIL_HEADS):
                onat[slot * DIL_HEADS + h, rows, :] = o_refs[gi][r, :, head_slice(h)]
        o_heads.append(lambda h, slot=slot: onat[slot * DIL_HEADS + h])
        ls.append(lnat[slot])
        slot += 1
    mx = functools.reduce(jnp.maximum, ls)
    es = [jnp.exp(l - mx) for l in ls]
    tot = functools.reduce(lambda u, v: u + v, es)
    ws = [e / tot for e in es]
    cols = []
    for h in range(DIL_HEADS):
        c = ws[0][:, h:h + 1] * o_heads[0](h)
        for gi in range(1, n_g):
            c = c + ws[gi][:, h:h + 1] * o_heads[gi](h)
        cols.append(c.astype(BF16))
    a = jnp.concatenate(cols, axis=1)
    out_ref[...] = h_ref[...] + jnp.dot(a, w_ref[...], preferred_element_type=F32)


def _dil_combine(os_, lses, w_o, h, *, seq, tm):
    t, d_model = h.shape
    w = w_o.shape[0]
    dilations = tuple(o.shape[1] for o in os_)
    per_b = seq // tm
    n_strided = sum(1 for d in dilations if d > 1)

    def res_spec(d, width):
        return pl.BlockSpec((None, d, tm // d, width), lambda i: (i // per_b, 0, i % per_b, 0))

    row = lambda i: (i, 0)
    return pl.pallas_call(
        functools.partial(_dil_combine_kernel, dilations=dilations),
        grid=(t // tm,),
        in_specs=[res_spec(d, w) for d in dilations] + [res_spec(d, LANES) for d in dilations]
                 + [pl.BlockSpec((w, d_model), lambda i: (0, 0)), pl.BlockSpec((tm, d_model), row)],
        out_specs=pl.BlockSpec((tm, d_model), row),
        out_shape=jax.ShapeDtypeStruct((t, d_model), F32),
        scratch_shapes=[pltpu.VMEM((n_strided * DIL_HEADS, tm, DIL_HEAD_DIM), F32),
                        pltpu.VMEM((n_strided, tm, LANES), F32)],
        compiler_params=_cparams("parallel"),
        name="dil_combine",
    )(*os_, *lses, w_o, h)


def _mem_kv_kernel(mem_ref, g_ref, w_ref, o_ref):
    xn = _rms(mem_ref[...], g_ref[...]).astype(BF16)
    o_ref[...] = jnp.dot(xn, w_ref[...], preferred_element_type=F32).astype(o_ref.dtype)


def _mem_kv(mem, g, w_kv):
    b, m, d = mem.shape
    depth, _, n = w_kv.shape
    return pl.pallas_call(
        _mem_kv_kernel,
        grid=(depth, b),
        in_specs=[pl.BlockSpec((None, m, d), lambda l, bb: (bb, 0, 0)),
                  pl.BlockSpec((1, d), lambda l, bb: (0, 0)),
                  pl.BlockSpec((None, d, n), lambda l, bb: (l, 0, 0))],
        out_specs=pl.BlockSpec((None, None, m, n), lambda l, bb: (l, bb, 0, 0)),
        out_shape=jax.ShapeDtypeStruct((depth, b, m, n), BF16),
        compiler_params=_cparams("parallel", "parallel"),
        name="mem_kv",
    )(mem, g.reshape(1, d), w_kv)


def _mem_attn_kernel(h_ref, g_ref, wq_ref, kv_ref, wo_ref, o_ref):
    x = h_ref[...]
    xn = _rms(x, g_ref[...]).astype(BF16)
    q = jnp.dot(xn, wq_ref[...], preferred_element_type=F32)
    width = MEM_HEADS * MEM_HEAD_DIM
    scale = MEM_HEAD_DIM ** -0.5
    outs = []
    for hh in range(MEM_HEADS):
        sl = slice(hh * MEM_HEAD_DIM, (hh + 1) * MEM_HEAD_DIM)
        k = kv_ref[:, sl]
        v = kv_ref[:, width + hh * MEM_HEAD_DIM:width + (hh + 1) * MEM_HEAD_DIM]
        sc = lax.dot_general(q[:, sl].astype(BF16), k, (((1,), (1,)), ((), ())),
                             preferred_element_type=F32) * scale
        mx = jnp.max(sc, axis=-1, keepdims=True)
        p = jnp.exp(sc - mx)
        p = p / jnp.sum(p, axis=-1, keepdims=True)
        outs.append(jnp.dot(p.astype(BF16), v, preferred_element_type=F32).astype(BF16))
    a = jnp.concatenate(outs, axis=1)
    o_ref[...] = x + jnp.dot(a, wo_ref[...], preferred_element_type=F32)


def _mem_attn(h, g, w_q, kv, w_o, layer, *, batch, seq, tm):
    t, d = h.shape
    width = MEM_HEADS * MEM_HEAD_DIM
    m = kv.shape[2]
    per_b = seq // tm
    return pl.pallas_call(
        _mem_attn_kernel,
        grid=(t // tm,),
        in_specs=[pl.BlockSpec((tm, d), lambda i: (i, 0)),
                  pl.BlockSpec((1, d), lambda i: (0, 0)),
                  pl.BlockSpec((None, d, width), lambda i: (layer, 0, 0)),
                  pl.BlockSpec((None, None, m, 2 * width), lambda i: (layer, i // per_b, 0, 0)),
                  pl.BlockSpec((None, width, d), lambda i: (layer, 0, 0))],
        out_specs=pl.BlockSpec((tm, d), lambda i: (i, 0)),
        out_shape=jax.ShapeDtypeStruct((t, d), F32),
        compiler_params=_cparams("parallel"),
        name="mem_attn",
    )(h, g.reshape(1, d), w_q, kv, w_o)


def _ffn_kernel(h_ref, g_ref, wg_ref, wu_ref, wo_ref, gf_ref, o_ref, xn_ref, *, final_norm):
    f = pl.program_id(1)

    @pl.when(f == 0)
    def _():
        x = h_ref[...]
        xn_ref[...] = _rms(x, g_ref[...]).astype(BF16)
        o_ref[...] = x

    xn = xn_ref[...]
    gate = jnp.dot(xn, wg_ref[...], preferred_element_type=F32)
    up = jnp.dot(xn, wu_ref[...], preferred_element_type=F32)
    a = (_silu(gate) * up).astype(BF16)
    o_ref[...] += jnp.dot(a, wo_ref[...], preferred_element_type=F32)

    if final_norm:
        @pl.when(f == pl.num_programs(1) - 1)
        def _():
            o_ref[...] = _rms(o_ref[...], gf_ref[...])


def _ffn(h, g, w_in, w_out, layer, g_final, *, tm, tf, final_norm):
    t, d = h.shape
    d_ff = w_out.shape[1]
    nf = d_ff // tf
    return pl.pallas_call(
        functools.partial(_ffn_kernel, final_norm=final_norm),
        grid=(t // tm, nf),
        in_specs=[pl.BlockSpec((tm, d), lambda i, f: (i, 0)),
                  pl.BlockSpec((1, d), lambda i, f: (0, 0)),
                  pl.BlockSpec((None, d, tf), lambda i, f: (layer, 0, f)),
                  pl.BlockSpec((None, d, tf), lambda i, f: (layer, 0, nf + f)),
                  pl.BlockSpec((None, tf, d), lambda i, f: (layer, f, 0)),
                  pl.BlockSpec((1, d), lambda i, f: (0, 0))],
        out_specs=pl.BlockSpec((tm, d), lambda i, f: (i, 0)),
        out_shape=jax.ShapeDtypeStruct((t, d), F32),
        scratch_shapes=[pltpu.VMEM((tm, d), BF16)],
        compiler_params=_cparams("parallel", "arbitrary"),
        name="ffn",
    )(h, g.reshape(1, d), w_in, w_in, w_out, g_final.reshape(1, d))


def kernel(x, mem, positions, norm_mix, norm_mem, norm_ffn, norm_final, ssd_w_in, ssd_conv_w, ssd_conv_b, ssd_dt_bias, ssd_a_log, ssd_d, ssd_norm, ssd_w_out, kv_norm, w_kv_shared, dil_w_q, dil_w_o, mem_src_norm, mem_w_q, mem_w_kv, mem_w_o, ffn_w_in, ffn_w_out):
    batch, seq, d_model = x.shape
    depth = norm_mix.shape[0]
    n_a = ssd_w_in.shape[0]
    d_inner = ssd_w_out.shape[1]
    heads = d_inner // SSD_HEAD_DIM
    zx_dim = ssd_w_in.shape[2] - heads
    dil_w = DIL_HEADS * DIL_HEAD_DIM
    n_dil = len(DIL_PATTERNS)
    t = batch * seq

    h = x.reshape(t, d_model)
    rope = _rope_tables(positions)
    mem_kv = _mem_kv(mem, mem_src_norm, mem_w_kv.astype(BF16))
    mem_w_q16 = mem_w_q.astype(BF16)
    mem_w_o16 = mem_w_o.astype(BF16)
    ffn_w_in16 = ffn_w_in.astype(BF16)
    ffn_w_out16 = ffn_w_out.astype(BF16)

    kv_sh = None
    for i in range(depth):
        if i < n_a:
            w_zx = ssd_w_in[i, :, :zx_dim].astype(BF16)
            w_dt = jnp.pad(ssd_w_in[i, :, zx_dim:], ((0, 0), (0, LANES - heads))).astype(BF16)
            proj = _norm_matmul(h, norm_mix[i], w_zx, F32, tm=1024, tn=512, name="ssd_in_proj")
            dt_raw = _norm_matmul(h, norm_mix[i], w_dt, F32, tm=1024, tn=LANES, name="ssd_dt_proj")
            y = _ssd_core(proj, dt_raw, ssd_conv_w[i], ssd_conv_b[i], ssd_dt_bias[i], ssd_a_log[i],
                          ssd_d[i], ssd_norm[i], seq=seq, d_inner=d_inner)
            h = _matmul_residual(y, ssd_w_out[i].astype(BF16), h, tm=512, tn=1024, name="ssd_out_proj")
        else:
            dils = tuple(d for _, d in DIL_PATTERNS)
            assert all(w // d == DIL_BLOCK for w, d in DIL_PATTERNS)
            if kv_sh is None:
                kv_sh = _norm_matmul_dil(h, kv_norm, w_kv_shared.astype(BF16), rope, dilations=dils * 2,
                                         rope_flags=(True,) * n_dil + (False,) * n_dil, batch=batch, seq=seq,
                                         tm=1024, tn=512, name="shared_kv_proj")
            j = i - n_a
            qs = _norm_matmul_dil(h, norm_mix[i], dil_w_q[j].astype(BF16), rope, dilations=dils,
                                  rope_flags=(True,) * n_dil, batch=batch, seq=seq, tm=1024, tn=512,
                                  name="dil_q_proj")
            outs, lses = [], []
            for g in range(n_dil):
                o, lse = _dil_attn(qs[g], kv_sh[g], kv_sh[n_dil + g])
                outs.append(o)
                lses.append(lse)
            h = _dil_combine(outs, lses, dil_w_o[j].astype(BF16), h, seq=seq, tm=256)
        h = _mem_attn(h, norm_mem[i], mem_w_q16, mem_kv, mem_w_o16, i, batch=batch, seq=seq, tm=512)
        h = _ffn(h, norm_ffn[i], ffn_w_in16, ffn_w_out16, i, norm_final, tm=512, tf=512,
                 final_norm=(i == depth - 1))
    return h.reshape(batch, seq, d_model)
```

```python
import functools
import math

import jax
import jax.numpy as jnp
import numpy as np
from jax import lax
from jax.experimental import pallas as pl
from jax.experimental.pallas import tpu as pltpu

F32 = jnp.float32
BF16 = jnp.bfloat16

NORM_EPS = 1e-6
SSD_HEAD_DIM = 64
SSD_GROUPS = 8
SSD_STATE = 128
SSD_CONV = 4
SSD_CHUNK = 128
DIL_PATTERNS = ((128, 1), (512, 4), (2048, 16))
DIL_HEADS = 16
DIL_HEAD_DIM = 128
DIL_BLOCK = 128
ROPE_THETA = 10000.0
LOG2E = math.log2(math.e)
MEM_HEADS = 4
MEM_HEAD_DIM = 128

LANES = 128
SUBLANES = 8
VMEM_LIMIT = 56 * 1024 * 1024
ROW_SUB = 256


def _cparams(*sem):
    return pltpu.CompilerParams(dimension_semantics=sem, vmem_limit_bytes=VMEM_LIMIT)


def _rms(x, g):
    ms = jnp.mean(x * x, axis=-1, keepdims=True)
    return x * lax.rsqrt(ms + NORM_EPS) * g


def _silu(x):
    return x / (1.0 + jnp.exp(-x))


def _rope_table_kernel(pos_ref, freq_ref, cos_ref, sin_ref):
    ang = pos_ref[...].astype(F32) * freq_ref[...]
    lane = lax.broadcasted_iota(jnp.int32, ang.shape, 1)
    s = jnp.sin(ang)
    cos_ref[...] = jnp.cos(ang)
    sin_ref[...] = jnp.where(lane < DIL_HEAD_DIM // 2, -s, s)


def _rope_tables(positions):
    t = positions.size
    half = DIL_HEAD_DIM // 2
    inv_freq = ROPE_THETA ** (-jnp.arange(half, dtype=F32) / half)
    freq = jnp.concatenate([inv_freq, inv_freq]).reshape(1, DIL_HEAD_DIM)
    tm = 1024
    return pl.pallas_call(
        _rope_table_kernel,
        grid=(t // tm,),
        in_specs=[pl.BlockSpec((tm, 1), lambda i: (i, 0)),
                  pl.BlockSpec((1, DIL_HEAD_DIM), lambda i: (0, 0))],
        out_specs=[pl.BlockSpec((tm, DIL_HEAD_DIM), lambda i: (i, 0))] * 2,
        out_shape=[jax.ShapeDtypeStruct((t, DIL_HEAD_DIM), F32)] * 2,
        compiler_params=_cparams("parallel"),
        name="rope_tables",
    )(positions.reshape(t, 1), freq)


def _normed_rows(x_ref, g_ref, xn_ref, rows, first):
    if first:
        xn = _rms(x_ref[rows, :], g_ref[...]).astype(BF16)
        xn_ref[rows, :] = xn
        return xn
    return xn_ref[rows, :]


def _row_subtiles(tm):
    return [slice(k * ROW_SUB, (k + 1) * ROW_SUB) for k in range(tm // ROW_SUB)]


def _first_step_variants(step_id, body):
    pl.when(step_id == 0)(functools.partial(body, True))
    pl.when(step_id != 0)(functools.partial(body, False))


def _norm_matmul_kernel(x_ref, g_ref, w_ref, o_ref, xn_ref):
    def body(first):
        for rows in _row_subtiles(x_ref.shape[0]):
            xn = _normed_rows(x_ref, g_ref, xn_ref, rows, first)
            o_ref[rows, :] = jnp.dot(xn, w_ref[...], preferred_element_type=F32).astype(o_ref.dtype)

    _first_step_variants(pl.program_id(1), body)


def _norm_matmul(x, g, w, out_dtype, *, tm, tn, name="norm_matmul"):
    t, d = x.shape
    n = w.shape[1]
    return pl.pallas_call(
        _norm_matmul_kernel,
        grid=(t // tm, n // tn),
        in_specs=[pl.BlockSpec((tm, d), lambda i, j: (i, 0)),
                  pl.BlockSpec((1, d), lambda i, j: (0, 0)),
                  pl.BlockSpec((d, tn), lambda i, j: (0, j))],
        out_specs=pl.BlockSpec((tm, tn), lambda i, j: (i, j)),
        out_shape=jax.ShapeDtypeStruct((t, n), out_dtype),
        scratch_shapes=[pltpu.VMEM((tm, d), BF16)],
        compiler_params=_cparams("parallel", "arbitrary"),
        name=name,
    )(x, g.reshape(1, d), w)


def _norm_matmul_dil_kernel(x_ref, g_ref, w_ref, cos_ref, sin_ref, o_ref, xn_ref, acc_ref, *, dilation, rope):
    d = dilation
    tn = w_ref.shape[1]

    def body(first):
        for k, rows in enumerate(_row_subtiles(x_ref.shape[0])):
            xn = _normed_rows(x_ref, g_ref, xn_ref, rows, first)
            acc = jnp.dot(xn, w_ref[...], preferred_element_type=F32)
            for hh in range(tn // DIL_HEAD_DIM):
                sl = slice(hh * DIL_HEAD_DIM, (hh + 1) * DIL_HEAD_DIM)
                t = acc[:, sl]
                if rope:
                    t = t * cos_ref[rows, :] + pltpu.roll(t, DIL_HEAD_DIM // 2, 1) * sin_ref[rows, :]
                if d == 1:
                    o_ref[0, rows, sl] = t.astype(BF16)
                    continue
                acc_ref[k, hh] = t
                per_res = ROW_SUB // d
                for r in range(d):
                    o_ref[r, k * per_res:(k + 1) * per_res, sl] = (
                        acc_ref[k, hh, pl.ds(r, per_res, stride=d), :].astype(BF16))

    _first_step_variants(pl.program_id(1), body)


def _norm_matmul_dil(x, g, w, rope_tables, *, col_block, dilation, rope, batch, seq, tm, tn, name):
    t, d_model = x.shape
    width = DIL_HEADS * DIL_HEAD_DIM
    tiles = width // tn
    per_b = seq // tm
    d = dilation
    return pl.pallas_call(
        functools.partial(_norm_matmul_dil_kernel, dilation=d, rope=rope),
        grid=(t // tm, tiles),
        in_specs=[pl.BlockSpec((tm, d_model), lambda i, j: (i, 0)),
                  pl.BlockSpec((1, d_model), lambda i, j: (0, 0)),
                  pl.BlockSpec((d_model, tn), lambda i, j: (0, col_block * tiles + j)),
                  pl.BlockSpec((tm, DIL_HEAD_DIM), lambda i, j: (i, 0)),
                  pl.BlockSpec((tm, DIL_HEAD_DIM), lambda i, j: (i, 0))],
        out_specs=pl.BlockSpec((None, d, tm // d, tn), lambda i, j: (i // per_b, 0, i % per_b, j)),
        out_shape=jax.ShapeDtypeStruct((batch, d, seq // d, width), BF16),
        scratch_shapes=[pltpu.VMEM((tm, d_model), BF16),
                        pltpu.VMEM((tm // ROW_SUB, tn // DIL_HEAD_DIM, ROW_SUB, DIL_HEAD_DIM), F32)],
        compiler_params=_cparams("parallel", "arbitrary"),
        name=name,
    )(x, g.reshape(1, d_model), w, *rope_tables)


def _matmul_residual_kernel(a_ref, w_ref, h_ref, o_ref):
    o_ref[...] = h_ref[...] + jnp.dot(a_ref[...], w_ref[...], preferred_element_type=F32)


def _matmul_residual(a, w, h, *, tm, tn, name="matmul_residual"):
    t, k = a.shape
    n = w.shape[1]
    return pl.pallas_call(
        _matmul_residual_kernel,
        grid=(n // tn, t // tm),
        in_specs=[pl.BlockSpec((tm, k), lambda j, i: (i, 0)),
                  pl.BlockSpec((k, tn), lambda j, i: (0, j)),
                  pl.BlockSpec((tm, tn), lambda j, i: (i, j))],
        out_specs=pl.BlockSpec((tm, tn), lambda j, i: (i, j)),
        out_shape=jax.ShapeDtypeStruct((t, n), F32),
        compiler_params=_cparams("parallel", "parallel"),
        name=name,
    )(a, w, h)


def _ssd_kernel(z_ref, xs_ref, b_ref, c_ref, dt_ref, convw_ref, convb_ref, dtb_ref, aneg_ref,
                dexp_ref, nw_ref, o_ref, xbuf, xc, state, *, chunks_per_seq):
    L = SSD_CHUNK
    d_inner = xs_ref.shape[1]
    bc_dim = b_ref.shape[1]
    conv_dim = d_inner + 2 * bc_dim
    gw = d_inner // SSD_GROUPS
    halo = SUBLANES
    first = pl.program_id(0) % chunks_per_seq == 0

    @pl.when(first)
    def _():
        xbuf[:, 0:halo, :] = jnp.zeros((xbuf.shape[0], halo, LANES), F32)
        state[...] = jnp.zeros(state.shape, F32)

    @pl.when(jnp.logical_not(first))
    def _():
        xbuf[:, 0:halo, :] = xbuf[:, L:L + halo, :]

    n_cb = conv_dim // LANES
    for cb in range(n_cb):
        c0 = cb * LANES
        if c0 < d_inner:
            src = xs_ref[:, c0:c0 + LANES]
        elif c0 < d_inner + bc_dim:
            src = b_ref[:, c0 - d_inner:c0 - d_inner + LANES]
        else:
            src = c_ref[:, c0 - d_inner - bc_dim:c0 - d_inner - bc_dim + LANES]
        xbuf[cb, halo:halo + L, :] = src

    nres = SUBLANES
    rows = L // nres
    for cb in range(n_cb):
        sl = slice(cb * LANES, (cb + 1) * LANES)
        taps = [convw_ref[k:k + 1, sl] for k in range(SSD_CONV)]
        bias = convb_ref[:, sl]
        cur = [xbuf[cb, pl.ds(halo + r, rows, stride=nres), :] for r in range(nres)]
        prv = {r: xbuf[cb, pl.ds(r, rows, stride=nres), :]
               for r in range(nres - SSD_CONV + 1, nres)}
        for r in range(nres):
            acc = bias
            for k in range(SSD_CONV):
                q = r - (SSD_CONV - 1) + k
                acc = acc + taps[k] * (cur[q] if q >= 0 else prv[q + nres])
            xc[cb, pl.ds(r, rows, stride=nres), :] = _silu(acc)

    raw = dt_ref[...] + dtb_ref[...]
    dt = jnp.maximum(raw, 0.0) + jnp.log1p(jnp.exp(-jnp.abs(raw)))
    acum = dt * aneg_ref[...]
    row = lax.broadcasted_iota(jnp.int32, (L, LANES), 0)
    sh = 1
    while sh < L:
        acum = acum + jnp.where(row >= sh, pltpu.roll(acum, sh, 0), 0.0)
        sh *= 2
    acum_t = acum.T
    dt_t = dt.T
    ea = jnp.exp(acum)
    wend = jnp.exp(acum[L - 1:L, :] - acum) * dt

    ti = lax.broadcasted_iota(jnp.int32, (L, L), 0)
    si = lax.broadcasted_iota(jnp.int32, (L, L), 1)
    causal = ti >= si
    lo_half = si < SSD_HEAD_DIM

    def pair_expand(v, h0):
        return jnp.where(lo_half, v[:, h0:h0 + 1], v[:, h0 + 1:h0 + 2])

    heads_per_group = gw // SSD_HEAD_DIM
    for g in range(SSD_GROUPS):
        bg = xc[(d_inner + g * SSD_STATE) // LANES]
        cg = xc[(d_inner + bc_dim + g * SSD_STATE) // LANES]
        cb16 = cg.astype(BF16)
        cb = lax.dot_general(cb16, bg.astype(BF16), (((1,), (1,)), ((), ())),
                             preferred_element_type=F32)
        bt16 = bg.T.astype(BF16)
        st_old = state[g]
        yoff = jnp.dot(cb16, st_old.astype(BF16), preferred_element_type=F32)
        ys, xws, cds = [], [], []
        for pp in range(heads_per_group // 2):
            h0 = g * heads_per_group + 2 * pp
            c0 = g * gw + pp * LANES
            xpair = xc[c0 // LANES]
            xp16 = xpair.astype(BF16)
            yd = []
            for hh in (h0, h0 + 1):
                seg = acum[:, hh:hh + 1] - acum_t[hh:hh + 1, :]
                dec = jnp.exp(jnp.where(causal, seg, -jnp.inf))
                m = (cb * dec * dt_t[hh:hh + 1, :]).astype(BF16)
                yd.append(jnp.dot(m, xp16, preferred_element_type=F32))
            ea_pair = pair_expand(ea, h0)
            y = (jnp.where(lo_half, yd[0], yd[1])
                 + yoff[:, pp * LANES:(pp + 1) * LANES] * ea_pair
                 + xpair * dexp_ref[:, c0:c0 + LANES])
            y = y * _silu(z_ref[:, c0:c0 + LANES])
            ys.append(y)
            xws.append((xpair * pair_expand(wend, h0)).astype(BF16))
            cds.append(ea_pair[L - 1:L, :])
        xw = jnp.concatenate(xws, axis=1)
        cd = jnp.concatenate(cds, axis=1)
        state[g] = st_old * cd + jnp.dot(bt16, xw, preferred_element_type=F32)
        yg = jnp.concatenate(ys, axis=1)
        ms = jnp.mean(yg * yg, axis=-1, keepdims=True)
        yn = yg * lax.rsqrt(ms + NORM_EPS) * nw_ref[:, g * gw:(g + 1) * gw]
        o_ref[:, g * gw:(g + 1) * gw] = yn.astype(o_ref.dtype)


def _ssd_core(proj, dt_raw, conv_w, conv_b, dt_bias, a_log, d_skip, norm_w, *, seq, d_inner):
    t = proj.shape[0]
    L = SSD_CHUNK
    bc_dim = SSD_GROUPS * SSD_STATE
    conv_dim = d_inner + 2 * bc_dim
    heads = d_inner // SSD_HEAD_DIM
    pad = LANES - heads

    def padrow(v):
        return jnp.pad(v.astype(F32), (0, pad)).reshape(1, LANES)

    aneg = padrow(-jnp.exp(a_log.astype(F32)))
    dexp = jnp.repeat(d_skip.astype(F32), SSD_HEAD_DIM).reshape(1, d_inner)
    xs_blk = d_inner // d_inner
    b_blk = (2 * d_inner) // bc_dim
    c_blk = b_blk + 1
    const = lambda c: (0, 0)
    return pl.pallas_call(
        functools.partial(_ssd_kernel, chunks_per_seq=seq // L),
        grid=(t // L,),
        in_specs=[pl.BlockSpec((L, d_inner), lambda c: (c, 0)),
                  pl.BlockSpec((L, d_inner), lambda c: (c, xs_blk)),
                  pl.BlockSpec((L, bc_dim), lambda c: (c, b_blk)),
                  pl.BlockSpec((L, bc_dim), lambda c: (c, c_blk)),
                  pl.BlockSpec((L, LANES), lambda c: (c, 0)),
                  pl.BlockSpec((SSD_CONV, conv_dim), const),
                  pl.BlockSpec((1, conv_dim), const),
                  pl.BlockSpec((1, LANES), const),
                  pl.BlockSpec((1, LANES), const),
                  pl.BlockSpec((1, d_inner), const),
                  pl.BlockSpec((1, d_inner), const)],
        out_specs=pl.BlockSpec((L, d_inner), lambda c: (c, 0)),
        out_shape=jax.ShapeDtypeStruct((t, d_inner), BF16),
        scratch_shapes=[pltpu.VMEM((conv_dim // LANES, L + 2 * SUBLANES, LANES), F32),
                        pltpu.VMEM((conv_dim // LANES, L, LANES), F32),
                        pltpu.VMEM((SSD_GROUPS, SSD_STATE, d_inner // SSD_GROUPS), F32)],
        compiler_params=_cparams("arbitrary"),
        name="ssd_core",
    )(proj, proj, proj, proj, dt_raw, conv_w.astype(F32), conv_b.reshape(1, conv_dim).astype(F32),
      padrow(dt_bias), aneg, dexp, norm_w.reshape(1, d_inner).astype(F32))


def _dil_attn_kernel(q_ref, k_ref, v_ref, o_ref, lse_ref, kprev, vprev, bias_ref):
    blk = DIL_BLOCK
    i = pl.program_id(1)

    @pl.when(i == 0)
    def _():
        kprev[...] = jnp.zeros(kprev.shape, kprev.dtype)
        vprev[...] = jnp.zeros(vprev.shape, vprev.dtype)

    qi = lax.broadcasted_iota(jnp.int32, (blk, 2 * blk), 0)
    ki = lax.broadcasted_iota(jnp.int32, (blk, 2 * blk), 1)
    dist = qi + blk - ki
    first_key = jnp.where(i > 0, 0, blk)
    valid = (dist >= 0) & (dist <= blk) & (ki >= first_key)
    bias_ref[...] = jnp.where(valid, 0.0, -jnp.inf)
    lane = lax.broadcasted_iota(jnp.int32, (blk, LANES), 1)
    scale = DIL_HEAD_DIM ** -0.5
    lse_tile = jnp.zeros((blk, LANES), F32)
    for h in range(DIL_HEADS):
        sl = slice(h * DIL_HEAD_DIM, (h + 1) * DIL_HEAD_DIM)
        k = jnp.concatenate([kprev[:, sl], k_ref[:, sl]], axis=0)
        v = jnp.concatenate([vprev[:, sl], v_ref[:, sl]], axis=0)
        raw = lax.dot_general(q_ref[:, sl], k, (((1,), (1,)), ((), ())), preferred_element_type=F32)
        raw = raw + bias_ref[...]
        mx = jnp.max(raw, axis=-1, keepdims=True)
        p = jnp.exp2((raw - mx) * (scale * LOG2E))
        den = jnp.sum(p, axis=-1, keepdims=True)
        o = jnp.dot(p.astype(BF16), v, preferred_element_type=F32)
        o_ref[:, sl] = o / den
        lse_tile = jnp.where(lane == h, mx * scale + jnp.log(den), lse_tile)
    lse_ref[...] = lse_tile
    kprev[...] = k_ref[...]
    vprev[...] = v_ref[...]


def _dil_attn(q, k, v):
    b, d, n, w = q.shape
    seqs = b * d
    blk = pl.BlockSpec((None, DIL_BLOCK, w), lambda s, i: (s, i, 0))
    o, lse = pl.pallas_call(
        _dil_attn_kernel,
        grid=(seqs, n // DIL_BLOCK),
        in_specs=[blk, blk, blk],
        out_specs=[blk, pl.BlockSpec((None, DIL_BLOCK, LANES), lambda s, i: (s, i, 0))],
        out_shape=[jax.ShapeDtypeStruct((seqs, n, w), F32),
                   jax.ShapeDtypeStruct((seqs, n, LANES), F32)],
        scratch_shapes=[pltpu.VMEM((DIL_BLOCK, w), BF16), pltpu.VMEM((DIL_BLOCK, w), BF16),
                        pltpu.VMEM((DIL_BLOCK, 2 * DIL_BLOCK), F32)],
        compiler_params=_cparams("arbitrary", "arbitrary"),
        name=f"dil_attn_d{d}",
    )(q.reshape(seqs, n, w), k.reshape(seqs, n, w), v.reshape(seqs, n, w))
    return o.reshape(b, d, n, w), lse.reshape(b, d, n, LANES)


def _dil_combine_kernel(*refs, dilations):
    n_g = len(dilations)
    o_refs, l_refs = refs[:n_g], refs[n_g:2 * n_g]
    w_ref, h_ref, out_ref, onat, lnat = refs[2 * n_g:]
    tm = h_ref.shape[0]

    def head_slice(h):
        return slice(h * DIL_HEAD_DIM, (h + 1) * DIL_HEAD_DIM)

    o_heads, ls = [], []
    slot = 0
    for gi, d in enumerate(dilations):
        if d == 1:
            o_heads.append(lambda h, rows, gi=gi: o_refs[gi][0, rows, head_slice(h)])
            ls.append(lambda rows, gi=gi: l_refs[gi][0, rows, :])
            continue
        for r in range(d):
            strided = pl.ds(r, tm // d, stride=d)
            lnat[slot, strided, :] = l_refs[gi][r]
            for h in range(DIL_HEADS):
                onat[slot * DIL_HEADS + h, strided, :] = o_refs[gi][r, :, head_slice(h)]
        o_heads.append(lambda h, rows, slot=slot: onat[slot * DIL_HEADS + h, rows, :])
        ls.append(lambda rows, slot=slot: lnat[slot, rows, :])
        slot += 1
    half = tm // 2
    for rows in (slice(0, half), slice(half, tm)):
        lv = [l(rows) for l in ls]
        mx = functools.reduce(jnp.maximum, lv)
        es = [jnp.exp(l - mx) for l in lv]
        tot = functools.reduce(lambda u, v: u + v, es)
        ws = [e / tot for e in es]
        cols = []
        for h in range(DIL_HEADS):
            c = ws[0][:, h:h + 1] * o_heads[0](h, rows)
            for gi in range(1, n_g):
                c = c + ws[gi][:, h:h + 1] * o_heads[gi](h, rows)
            cols.append(c.astype(BF16))
        a = jnp.concatenate(cols, axis=1)
        out_ref[rows, :] = h_ref[rows, :] + jnp.dot(a, w_ref[...], preferred_element_type=F32)


def _dil_combine(os_, lses, w_o, h, *, seq, tm):
    t, d_model = h.shape
    w = w_o.shape[0]
    dilations = tuple(o.shape[1] for o in os_)
    per_b = seq // tm
    n_strided = sum(1 for d in dilations if d > 1)

    def res_spec(d, width):
        return pl.BlockSpec((None, d, tm // d, width), lambda i: (i // per_b, 0, i % per_b, 0))

    row = lambda i: (i, 0)
    return pl.pallas_call(
        functools.partial(_dil_combine_kernel, dilations=dilations),
        grid=(t // tm,),
        in_specs=[res_spec(d, w) for d in dilations] + [res_spec(d, LANES) for d in dilations]
                 + [pl.BlockSpec((w, d_model), lambda i: (0, 0)), pl.BlockSpec((tm, d_model), row)],
        out_specs=pl.BlockSpec((tm, d_model), row),
        out_shape=jax.ShapeDtypeStruct((t, d_model), F32),
        scratch_shapes=[pltpu.VMEM((n_strided * DIL_HEADS, tm, DIL_HEAD_DIM), F32),
                        pltpu.VMEM((n_strided, tm, LANES), F32)],
        compiler_params=_cparams("parallel"),
        name="dil_combine",
    )(*os_, *lses, w_o, h)


def _mem_kv_kernel(mem_ref, g_ref, w_ref, o_ref):
    xn = _rms(mem_ref[...], g_ref[...]).astype(BF16)
    o_ref[...] = jnp.dot(xn, w_ref[...], preferred_element_type=F32).astype(o_ref.dtype)


def _mem_kv(mem, g, w_kv):
    b, m, d = mem.shape
    depth, _, n = w_kv.shape
    return pl.pallas_call(
        _mem_kv_kernel,
        grid=(depth, b),
        in_specs=[pl.BlockSpec((None, m, d), lambda l, bb: (bb, 0, 0)),
                  pl.BlockSpec((1, d), lambda l, bb: (0, 0)),
                  pl.BlockSpec((None, d, n), lambda l, bb: (l, 0, 0))],
        out_specs=pl.BlockSpec((None, None, m, n), lambda l, bb: (l, bb, 0, 0)),
        out_shape=jax.ShapeDtypeStruct((depth, b, m, n), BF16),
        compiler_params=_cparams("parallel", "parallel"),
        name="mem_kv",
    )(mem, g.reshape(1, d), w_kv)


def _mem_attn_kernel(h_ref, g_ref, wq_ref, kv_ref, wo_ref, o_ref):
    width = MEM_HEADS * MEM_HEAD_DIM
    scale = MEM_HEAD_DIM ** -0.5
    x = h_ref[...]
    xn = _rms(x, g_ref[...]).astype(BF16)
    q = jnp.dot(xn, wq_ref[...], preferred_element_type=F32)
    outs = []
    for hh in range(MEM_HEADS):
        sl = slice(hh * MEM_HEAD_DIM, (hh + 1) * MEM_HEAD_DIM)
        k = kv_ref[:, sl]
        v = kv_ref[:, width + hh * MEM_HEAD_DIM:width + (hh + 1) * MEM_HEAD_DIM]
        sc = lax.dot_general(q[:, sl].astype(BF16), k, (((1,), (1,)), ((), ())),
                             preferred_element_type=F32) * scale
        mx = jnp.max(sc, axis=-1, keepdims=True)
        p = jnp.exp(sc - mx)
        p = p / jnp.sum(p, axis=-1, keepdims=True)
        outs.append(jnp.dot(p.astype(BF16), v, preferred_element_type=F32).astype(BF16))
    a = jnp.concatenate(outs, axis=1)
    o_ref[...] = x + jnp.dot(a, wo_ref[...], preferred_element_type=F32)


def _mem_attn(h, g, w_q, kv, w_o, layer, *, batch, seq, tm):
    t, d = h.shape
    width = MEM_HEADS * MEM_HEAD_DIM
    m = kv.shape[2]
    per_b = seq // tm
    return pl.pallas_call(
        _mem_attn_kernel,
        grid=(t // tm,),
        in_specs=[pl.BlockSpec((tm, d), lambda i: (i, 0)),
                  pl.BlockSpec((1, d), lambda i: (0, 0)),
                  pl.BlockSpec((None, d, width), lambda i: (layer, 0, 0)),
                  pl.BlockSpec((None, None, m, 2 * width), lambda i: (layer, i // per_b, 0, 0)),
                  pl.BlockSpec((None, width, d), lambda i: (layer, 0, 0))],
        out_specs=pl.BlockSpec((tm, d), lambda i: (i, 0)),
        out_shape=jax.ShapeDtypeStruct((t, d), F32),
        compiler_params=_cparams("parallel"),
        name="mem_attn",
    )(h, g.reshape(1, d), w_q, kv, w_o)


def _ffn_kernel(h_ref, g_ref, wg_ref, wu_ref, wo_ref, gf_ref, o_ref, xn_ref, *, final_norm):
    f = pl.program_id(1)

    def body(first):
        for rows in _row_subtiles(h_ref.shape[0]):
            xn = _normed_rows(h_ref, g_ref, xn_ref, rows, first)
            gate = jnp.dot(xn, wg_ref[...], preferred_element_type=F32)
            up = jnp.dot(xn, wu_ref[...], preferred_element_type=F32)
            a = (_silu(gate) * up).astype(BF16)
            base = h_ref[rows, :] if first else o_ref[rows, :]
            o_ref[rows, :] = base + jnp.dot(a, wo_ref[...], preferred_element_type=F32)

    _first_step_variants(f, body)

    if final_norm:
        @pl.when(f == pl.num_programs(1) - 1)
        def _():
            o_ref[...] = _rms(o_ref[...], gf_ref[...])


def _ffn(h, g, w_in, w_out, layer, g_final, *, tm, tf, final_norm):
    t, d = h.shape
    d_ff = w_out.shape[1]
    nf = d_ff // tf
    return pl.pallas_call(
        functools.partial(_ffn_kernel, final_norm=final_norm),
        grid=(t // tm, nf),
        in_specs=[pl.BlockSpec((tm, d), lambda i, f: (i, 0)),
                  pl.BlockSpec((1, d), lambda i, f: (0, 0)),
                  pl.BlockSpec((None, d, tf), lambda i, f: (layer, 0, f)),
                  pl.BlockSpec((None, d, tf), lambda i, f: (layer, 0, nf + f)),
                  pl.BlockSpec((None, tf, d), lambda i, f: (layer, f, 0)),
                  pl.BlockSpec((1, d), lambda i, f: (0, 0))],
        out_specs=pl.BlockSpec((tm, d), lambda i, f: (i, 0)),
        out_shape=jax.ShapeDtypeStruct((t, d), F32),
        scratch_shapes=[pltpu.VMEM((tm, d), BF16)],
        compiler_params=_cparams("parallel", "arbitrary"),
        name="ffn",
    )(h, g.reshape(1, d), w_in, w_in, w_out, g_final.reshape(1, d))


def kernel(x, mem, positions, norm_mix, norm_mem, norm_ffn, norm_final, ssd_w_in, ssd_conv_w, ssd_conv_b, ssd_dt_bias, ssd_a_log, ssd_d, ssd_norm, ssd_w_out, kv_norm, w_kv_shared, dil_w_q, dil_w_o, mem_src_norm, mem_w_q, mem_w_kv, mem_w_o, ffn_w_in, ffn_w_out):
    batch, seq, d_model = x.shape
    depth = norm_mix.shape[0]
    n_a = ssd_w_in.shape[0]
    d_inner = ssd_w_out.shape[1]
    heads = d_inner // SSD_HEAD_DIM
    zx_dim = ssd_w_in.shape[2] - heads
    dil_w = DIL_HEADS * DIL_HEAD_DIM
    n_dil = len(DIL_PATTERNS)
    t = batch * seq

    h = x.reshape(t, d_model)
    rope = _rope_tables(positions)
    mem_kv = _mem_kv(mem, mem_src_norm, mem_w_kv.astype(BF16))
    mem_w_q16 = mem_w_q.astype(BF16)
    mem_w_o16 = mem_w_o.astype(BF16)
    ffn_w_in16 = ffn_w_in.astype(BF16)
    ffn_w_out16 = ffn_w_out.astype(BF16)

    kv_sh = None
    for i in range(depth):
        if i < n_a:
            w_zx = ssd_w_in[i, :, :zx_dim].astype(BF16)
            w_dt = jnp.pad(ssd_w_in[i, :, zx_dim:], ((0, 0), (0, LANES - heads))).astype(BF16)
            proj = _norm_matmul(h, norm_mix[i], w_zx, F32, tm=1024, tn=1024, name="ssd_in_proj")
            dt_raw = _norm_matmul(h, norm_mix[i], w_dt, F32, tm=1024, tn=LANES, name="ssd_dt_proj")
            y = _ssd_core(proj, dt_raw, ssd_conv_w[i], ssd_conv_b[i], ssd_dt_bias[i], ssd_a_log[i],
                          ssd_d[i], ssd_norm[i], seq=seq, d_inner=d_inner)
            h = _matmul_residual(y, ssd_w_out[i].astype(BF16), h, tm=512, tn=1024, name="ssd_out_proj")
        else:
            dils = tuple(d for _, d in DIL_PATTERNS)
            assert all(w // d == DIL_BLOCK for w, d in DIL_PATTERNS)
            proj = functools.partial(_norm_matmul_dil, batch=batch, seq=seq, tm=1024, tn=1024)
            if kv_sh is None:
                w_kv16 = w_kv_shared.astype(BF16)
                kv_sh = [proj(h, kv_norm, w_kv16, rope, col_block=c, dilation=dils[c % n_dil],
                              rope=c < n_dil, name=f"shared_kv_proj_{c}") for c in range(2 * n_dil)]
            j = i - n_a
            w_q16 = dil_w_q[j].astype(BF16)
            qs = [proj(h, norm_mix[i], w_q16, rope, col_block=c, dilation=dils[c], rope=True,
                       name=f"dil_q_proj_{c}") for c in range(n_dil)]
            outs, lses = [], []
            for g in range(n_dil):
                o, lse = _dil_attn(qs[g], kv_sh[g], kv_sh[n_dil + g])
                outs.append(o)
                lses.append(lse)
            h = _dil_combine(outs, lses, dil_w_o[j].astype(BF16), h, seq=seq, tm=256)
        h = _mem_attn(h, norm_mem[i], mem_w_q16, mem_kv, mem_w_o16, i, batch=batch, seq=seq, tm=512)
        h = _ffn(h, norm_ffn[i], ffn_w_in16, ffn_w_out16, i, norm_final, tm=1024, tf=512,
                 final_norm=(i == depth - 1))
    return h.reshape(batch, seq, d_model)
```

```python
import functools
import math

import jax
import jax.numpy as jnp
import numpy as np
from jax import lax
from jax.experimental import pallas as pl
from jax.experimental.pallas import tpu as pltpu

F32 = jnp.float32
BF16 = jnp.bfloat16

NORM_EPS = 1e-6
SSD_HEAD_DIM = 64
SSD_GROUPS = 8
SSD_STATE = 128
SSD_CONV = 4
SSD_CHUNK = 128
DIL_PATTERNS = ((128, 1), (512, 4), (2048, 16))
DIL_HEADS = 16
DIL_HEAD_DIM = 128
DIL_BLOCK = 128
ROPE_THETA = 10000.0
LOG2E = math.log2(math.e)
MEM_HEADS = 4
MEM_HEAD_DIM = 128

LANES = 128
SUBLANES = 8
VMEM_LIMIT = 56 * 1024 * 1024
ROW_SUB = 256
ATTN_BLOCKS_PER_STEP = 4


def _cparams(*sem):
    return pltpu.CompilerParams(dimension_semantics=sem, vmem_limit_bytes=VMEM_LIMIT)


def _rms(x, g):
    ms = jnp.mean(x * x, axis=-1, keepdims=True)
    return x * lax.rsqrt(ms + NORM_EPS) * g


def _silu(x):
    return x / (1.0 + jnp.exp(-x))


def _rope_table_kernel(pos_ref, freq_ref, cos_ref, sin_ref):
    ang = pos_ref[...].astype(F32) * freq_ref[...]
    lane = lax.broadcasted_iota(jnp.int32, ang.shape, 1)
    s = jnp.sin(ang)
    cos_ref[...] = jnp.cos(ang)
    sin_ref[...] = jnp.where(lane < DIL_HEAD_DIM // 2, -s, s)


def _rope_tables(positions):
    t = positions.size
    half = DIL_HEAD_DIM // 2
    inv_freq = ROPE_THETA ** (-jnp.arange(half, dtype=F32) / half)
    freq = jnp.concatenate([inv_freq, inv_freq]).reshape(1, DIL_HEAD_DIM)
    tm = 1024
    return pl.pallas_call(
        _rope_table_kernel,
        grid=(t // tm,),
        in_specs=[pl.BlockSpec((tm, 1), lambda i: (i, 0)),
                  pl.BlockSpec((1, DIL_HEAD_DIM), lambda i: (0, 0))],
        out_specs=[pl.BlockSpec((tm, DIL_HEAD_DIM), lambda i: (i, 0))] * 2,
        out_shape=[jax.ShapeDtypeStruct((t, DIL_HEAD_DIM), F32)] * 2,
        compiler_params=_cparams("parallel"),
        name="rope_tables",
    )(positions.reshape(t, 1), freq)


def _normed_rows(x_ref, g_ref, xn_ref, rows, first):
    if first:
        xn = _rms(x_ref[rows, :], g_ref[...]).astype(BF16)
        xn_ref[rows, :] = xn
        return xn
    return xn_ref[rows, :]


def _row_subtiles(tm):
    return [slice(k * ROW_SUB, (k + 1) * ROW_SUB) for k in range(tm // ROW_SUB)]


def _first_step_variants(step_id, body):
    pl.when(step_id == 0)(functools.partial(body, True))
    pl.when(step_id != 0)(functools.partial(body, False))


def _ssd_in_proj_kernel(x_ref, g_ref, w_ref, wdt_ref, o_ref, dt_ref, xn_ref):
    def body(first):
        for rows in _row_subtiles(x_ref.shape[0]):
            xn = _normed_rows(x_ref, g_ref, xn_ref, rows, first)
            o_ref[rows, :] = jnp.dot(xn, w_ref[...], preferred_element_type=F32)
            if first:
                dt_ref[rows, :] = jnp.dot(xn, wdt_ref[...], preferred_element_type=F32)

    _first_step_variants(pl.program_id(1), body)


def _ssd_in_proj(x, g, w, w_dt, layer, *, n_cols, tm, tn):
    t, d = x.shape
    return pl.pallas_call(
        _ssd_in_proj_kernel,
        grid=(t // tm, n_cols // tn),
        in_specs=[pl.BlockSpec((tm, d), lambda i, j: (i, 0)),
                  pl.BlockSpec((1, d), lambda i, j: (0, 0)),
                  pl.BlockSpec((None, d, tn), lambda i, j: (layer, 0, j)),
                  pl.BlockSpec((None, d, LANES), lambda i, j: (layer, 0, 0))],
        out_specs=[pl.BlockSpec((tm, tn), lambda i, j: (i, j)),
                   pl.BlockSpec((tm, LANES), lambda i, j: (i, 0))],
        out_shape=[jax.ShapeDtypeStruct((t, n_cols), F32), jax.ShapeDtypeStruct((t, LANES), F32)],
        scratch_shapes=[pltpu.VMEM((tm, d), BF16)],
        compiler_params=_cparams("parallel", "arbitrary"),
        name="ssd_in_proj",
    )(x, g.reshape(1, d), w, w_dt)


def _norm_matmul_dil_kernel(x_ref, g_ref, w_ref, cos_ref, sin_ref, o_ref, xn_ref, acc_ref, *, dilation, rope):
    d = dilation
    tn = w_ref.shape[1]

    def body(first):
        for k, rows in enumerate(_row_subtiles(x_ref.shape[0])):
            xn = _normed_rows(x_ref, g_ref, xn_ref, rows, first)
            acc = jnp.dot(xn, w_ref[...], preferred_element_type=F32)
            for hh in range(tn // DIL_HEAD_DIM):
                sl = slice(hh * DIL_HEAD_DIM, (hh + 1) * DIL_HEAD_DIM)
                t = acc[:, sl]
                if rope:
                    t = t * cos_ref[rows, :] + pltpu.roll(t, DIL_HEAD_DIM // 2, 1) * sin_ref[rows, :]
                if d == 1:
                    o_ref[0, rows, sl] = t.astype(BF16)
                    continue
                acc_ref[k, hh] = t
                per_res = ROW_SUB // d
                for r in range(d):
                    o_ref[r, k * per_res:(k + 1) * per_res, sl] = (
                        acc_ref[k, hh, pl.ds(r, per_res, stride=d), :].astype(BF16))

    _first_step_variants(pl.program_id(1), body)


def _norm_matmul_dil(x, g, w, rope_tables, *, col_block, dilation, rope, batch, seq, tm, tn, name):
    t, d_model = x.shape
    width = DIL_HEADS * DIL_HEAD_DIM
    tiles = width // tn
    per_b = seq // tm
    d = dilation
    return pl.pallas_call(
        functools.partial(_norm_matmul_dil_kernel, dilation=d, rope=rope),
        grid=(t // tm, tiles),
        in_specs=[pl.BlockSpec((tm, d_model), lambda i, j: (i, 0)),
                  pl.BlockSpec((1, d_model), lambda i, j: (0, 0)),
                  pl.BlockSpec((d_model, tn), lambda i, j: (0, col_block * tiles + j)),
                  pl.BlockSpec((tm, DIL_HEAD_DIM), lambda i, j: (i, 0)),
                  pl.BlockSpec((tm, DIL_HEAD_DIM), lambda i, j: (i, 0))],
        out_specs=pl.BlockSpec((None, d, tm // d, tn), lambda i, j: (i // per_b, 0, i % per_b, j)),
        out_shape=jax.ShapeDtypeStruct((batch, d, seq // d, width), BF16),
        scratch_shapes=[pltpu.VMEM((tm, d_model), BF16),
                        pltpu.VMEM((tm // ROW_SUB, tn // DIL_HEAD_DIM, ROW_SUB, DIL_HEAD_DIM), F32)],
        compiler_params=_cparams("parallel", "arbitrary"),
        name=name,
    )(x, g.reshape(1, d_model), w, *rope_tables)


def _matmul_residual_kernel(a_ref, w_ref, h_ref, o_ref):
    o_ref[...] = h_ref[...] + jnp.dot(a_ref[...], w_ref[...], preferred_element_type=F32)


def _matmul_residual(a, w, h, *, tm, tn, name="matmul_residual"):
    t, k = a.shape
    n = w.shape[1]
    return pl.pallas_call(
        _matmul_residual_kernel,
        grid=(n // tn, t // tm),
        in_specs=[pl.BlockSpec((tm, k), lambda j, i: (i, 0)),
                  pl.BlockSpec((k, tn), lambda j, i: (0, j)),
                  pl.BlockSpec((tm, tn), lambda j, i: (i, j))],
        out_specs=pl.BlockSpec((tm, tn), lambda j, i: (i, j)),
        out_shape=jax.ShapeDtypeStruct((t, n), F32),
        compiler_params=_cparams("parallel", "parallel"),
        name=name,
    )(a, w, h)


def _ssd_kernel(z_ref, xs_ref, b_ref, c_ref, dt_ref, convw_ref, convb_ref, dtb_ref, aneg_ref,
                dexp_ref, nw_ref, o_ref, xbuf, xc, state, *, chunks_per_seq):
    L = SSD_CHUNK
    d_inner = xs_ref.shape[1]
    bc_dim = b_ref.shape[1]
    conv_dim = d_inner + 2 * bc_dim
    gw = d_inner // SSD_GROUPS
    halo = SUBLANES
    first = pl.program_id(0) % chunks_per_seq == 0

    @pl.when(first)
    def _():
        xbuf[:, 0:halo, :] = jnp.zeros((xbuf.shape[0], halo, LANES), F32)
        state[...] = jnp.zeros(state.shape, F32)

    @pl.when(jnp.logical_not(first))
    def _():
        xbuf[:, 0:halo, :] = xbuf[:, L:L + halo, :]

    n_cb = conv_dim // LANES
    for cb in range(n_cb):
        c0 = cb * LANES
        if c0 < d_inner:
            src = xs_ref[:, c0:c0 + LANES]
        elif c0 < d_inner + bc_dim:
            src = b_ref[:, c0 - d_inner:c0 - d_inner + LANES]
        else:
            src = c_ref[:, c0 - d_inner - bc_dim:c0 - d_inner - bc_dim + LANES]
        xbuf[cb, halo:halo + L, :] = src

    nres = SUBLANES
    rows = L // nres
    for cb in range(n_cb):
        sl = slice(cb * LANES, (cb + 1) * LANES)
        taps = [convw_ref[k:k + 1, sl] for k in range(SSD_CONV)]
        bias = convb_ref[:, sl]
        cur = [xbuf[cb, pl.ds(halo + r, rows, stride=nres), :] for r in range(nres)]
        prv = {r: xbuf[cb, pl.ds(r, rows, stride=nres), :]
               for r in range(nres - SSD_CONV + 1, nres)}
        for r in range(nres):
            acc = bias
            for k in range(SSD_CONV):
                q = r - (SSD_CONV - 1) + k
                acc = acc + taps[k] * (cur[q] if q >= 0 else prv[q + nres])
            xc[cb, pl.ds(r, rows, stride=nres), :] = _silu(acc)

    raw = dt_ref[...] + dtb_ref[...]
    dt = jnp.maximum(raw, 0.0) + jnp.log1p(jnp.exp(-jnp.abs(raw)))
    acum = dt * aneg_ref[...]
    row = lax.broadcasted_iota(jnp.int32, (L, LANES), 0)
    sh = 1
    while sh < L:
        acum = acum + jnp.where(row >= sh, pltpu.roll(acum, sh, 0), 0.0)
        sh *= 2
    acum2 = acum * LOG2E
    acum2_t = acum2.T
    dt_t = dt.T
    ea = jnp.exp(acum)
    wend = jnp.exp(acum[L - 1:L, :] - acum) * dt

    ti = lax.broadcasted_iota(jnp.int32, (L, L), 0)
    si = lax.broadcasted_iota(jnp.int32, (L, L), 1)
    causal = ti >= si
    lo_half = si < SSD_HEAD_DIM

    def pair_expand(v, h0):
        return jnp.where(lo_half, v[:, h0:h0 + 1], v[:, h0 + 1:h0 + 2])

    heads_per_group = gw // SSD_HEAD_DIM
    for g in range(SSD_GROUPS):
        bg = xc[(d_inner + g * SSD_STATE) // LANES]
        cg = xc[(d_inner + bc_dim + g * SSD_STATE) // LANES]
        cb16 = cg.astype(BF16)
        cb = lax.dot_general(cb16, bg.astype(BF16), (((1,), (1,)), ((), ())),
                             preferred_element_type=F32)
        bt16 = bg.T.astype(BF16)
        st_old = state[g]
        yoff = jnp.dot(cb16, st_old.astype(BF16), preferred_element_type=F32)
        ys, xws, cds = [], [], []
        for pp in range(heads_per_group // 2):
            h0 = g * heads_per_group + 2 * pp
            c0 = g * gw + pp * LANES
            xpair = xc[c0 // LANES]
            xp16 = xpair.astype(BF16)
            yd = []
            for hh in (h0, h0 + 1):
                seg = acum2[:, hh:hh + 1] - acum2_t[hh:hh + 1, :]
                dec = jnp.exp2(jnp.where(causal, seg, -jnp.inf))
                m = (cb * dec * dt_t[hh:hh + 1, :]).astype(BF16)
                yd.append(jnp.dot(m, xp16, preferred_element_type=F32))
            ea_pair = pair_expand(ea, h0)
            y = (jnp.where(lo_half, yd[0], yd[1])
                 + yoff[:, pp * LANES:(pp + 1) * LANES] * ea_pair
                 + xpair * dexp_ref[:, c0:c0 + LANES])
            y = y * _silu(z_ref[:, c0:c0 + LANES])
            ys.append(y)
            xws.append((xpair * pair_expand(wend, h0)).astype(BF16))
            cds.append(ea_pair[L - 1:L, :])
        xw = jnp.concatenate(xws, axis=1)
        cd = jnp.concatenate(cds, axis=1)
        state[g] = st_old * cd + jnp.dot(bt16, xw, preferred_element_type=F32)
        yg = jnp.concatenate(ys, axis=1)
        ms = jnp.mean(yg * yg, axis=-1, keepdims=True)
        yn = yg * lax.rsqrt(ms + NORM_EPS) * nw_ref[:, g * gw:(g + 1) * gw]
        o_ref[:, g * gw:(g + 1) * gw] = yn.astype(o_ref.dtype)


def _ssd_core(proj, dt_raw, conv_w, conv_b, dt_bias, a_log, d_skip, norm_w, *, seq, d_inner):
    t = proj.shape[0]
    L = SSD_CHUNK
    bc_dim = SSD_GROUPS * SSD_STATE
    conv_dim = d_inner + 2 * bc_dim
    heads = d_inner // SSD_HEAD_DIM
    pad = LANES - heads

    def padrow(v):
        return jnp.pad(v.astype(F32), (0, pad)).reshape(1, LANES)

    aneg = padrow(-jnp.exp(a_log.astype(F32)))
    dexp = jnp.repeat(d_skip.astype(F32), SSD_HEAD_DIM).reshape(1, d_inner)
    xs_blk = d_inner // d_inner
    b_blk = (2 * d_inner) // bc_dim
    c_blk = b_blk + 1
    const = lambda c: (0, 0)
    return pl.pallas_call(
        functools.partial(_ssd_kernel, chunks_per_seq=seq // L),
        grid=(t // L,),
        in_specs=[pl.BlockSpec((L, d_inner), lambda c: (c, 0)),
                  pl.BlockSpec((L, d_inner), lambda c: (c, xs_blk)),
                  pl.BlockSpec((L, bc_dim), lambda c: (c, b_blk)),
                  pl.BlockSpec((L, bc_dim), lambda c: (c, c_blk)),
                  pl.BlockSpec((L, LANES), lambda c: (c, 0)),
                  pl.BlockSpec((SSD_CONV, conv_dim), const),
                  pl.BlockSpec((1, conv_dim), const),
                  pl.BlockSpec((1, LANES), const),
                  pl.BlockSpec((1, LANES), const),
                  pl.BlockSpec((1, d_inner), const),
                  pl.BlockSpec((1, d_inner), const)],
        out_specs=pl.BlockSpec((L, d_inner), lambda c: (c, 0)),
        out_shape=jax.ShapeDtypeStruct((t, d_inner), BF16),
        scratch_shapes=[pltpu.VMEM((conv_dim // LANES, L + 2 * SUBLANES, LANES), F32),
                        pltpu.VMEM((conv_dim // LANES, L, LANES), F32),
                        pltpu.VMEM((SSD_GROUPS, SSD_STATE, d_inner // SSD_GROUPS), F32)],
        compiler_params=_cparams("arbitrary"),
        name="ssd_core",
    )(proj, proj, proj, proj, dt_raw, conv_w.astype(F32), conv_b.reshape(1, conv_dim).astype(F32),
      padrow(dt_bias), aneg, dexp, norm_w.reshape(1, d_inner).astype(F32))


def _dil_attn_kernel(q_ref, k_ref, v_ref, o_ref, st_ref, kprev, vprev, bias_ref):
    blk = DIL_BLOCK
    n_blk = q_ref.shape[0] // blk
    i = pl.program_id(1)

    @pl.when(i == 0)
    def _():
        kprev[...] = jnp.zeros(kprev.shape, kprev.dtype)
        vprev[...] = jnp.zeros(vprev.shape, vprev.dtype)

    qi = lax.broadcasted_iota(jnp.int32, (blk, 2 * blk), 0)
    ki = lax.broadcasted_iota(jnp.int32, (blk, 2 * blk), 1)
    dist = qi + blk - ki
    band = (dist >= 0) & (dist <= blk)
    first_key = jnp.where(i > 0, 0, blk)
    bias_ref[0] = jnp.where(band & (ki >= first_key), 0.0, -jnp.inf)
    bias_ref[1] = jnp.where(band, 0.0, -jnp.inf)
    st_ref[...] = jnp.zeros(st_ref.shape, F32)
    scale = DIL_HEAD_DIM ** -0.5
    for j in range(n_blk):
        rows = slice(j * blk, (j + 1) * blk)
        prev_rows = slice((j - 1) * blk, j * blk)
        for h in range(DIL_HEADS):
            sl = slice(h * DIL_HEAD_DIM, (h + 1) * DIL_HEAD_DIM)
            k_prev = kprev[:, sl] if j == 0 else k_ref[prev_rows, sl]
            v_prev = vprev[:, sl] if j == 0 else v_ref[prev_rows, sl]
            k = jnp.concatenate([k_prev, k_ref[rows, sl]], axis=0)
            v = jnp.concatenate([v_prev, v_ref[rows, sl]], axis=0)
            raw = lax.dot_general(q_ref[rows, sl], k, (((1,), (1,)), ((), ())), preferred_element_type=F32)
            raw = raw + bias_ref[min(j, 1)]
            mx = jnp.max(raw, axis=-1, keepdims=True)
            p = jnp.exp2((raw - mx) * (scale * LOG2E))
            o_ref[rows, sl] = jnp.dot(p.astype(BF16), v, preferred_element_type=F32)
            st_ref[rows, h:h + 1] = mx * scale
            st_ref[rows, DIL_HEADS + h:DIL_HEADS + h + 1] = jnp.sum(p, axis=-1, keepdims=True)
    last = slice((n_blk - 1) * blk, n_blk * blk)
    kprev[...] = k_ref[last, :]
    vprev[...] = v_ref[last, :]


def _dil_attn(q, k, v):
    b, d, n, w = q.shape
    seqs = b * d
    rows = min(n, ATTN_BLOCKS_PER_STEP * DIL_BLOCK)
    blk = pl.BlockSpec((None, rows, w), lambda s, i: (s, i, 0))
    o, st = pl.pallas_call(
        _dil_attn_kernel,
        grid=(seqs, n // rows),
        in_specs=[blk, blk, blk],
        out_specs=[blk, pl.BlockSpec((None, rows, LANES), lambda s, i: (s, i, 0))],
        out_shape=[jax.ShapeDtypeStruct((seqs, n, w), F32),
                   jax.ShapeDtypeStruct((seqs, n, LANES), F32)],
        scratch_shapes=[pltpu.VMEM((DIL_BLOCK, w), BF16), pltpu.VMEM((DIL_BLOCK, w), BF16),
                        pltpu.VMEM((2, DIL_BLOCK, 2 * DIL_BLOCK), F32)],
        compiler_params=_cparams("arbitrary", "arbitrary"),
        name=f"dil_attn_d{d}",
    )(q.reshape(seqs, n, w), k.reshape(seqs, n, w), v.reshape(seqs, n, w))
    return o.reshape(b, d, n, w), st.reshape(b, d, n, LANES)


def _dil_combine_kernel(*refs, dilations):
    n_g = len(dilations)
    o_refs, l_refs = refs[:n_g], refs[n_g:2 * n_g]
    w_ref, h_ref, out_ref, onat, lnat = refs[2 * n_g:]
    tm = h_ref.shape[0]

    def head_slice(h):
        return slice(h * DIL_HEAD_DIM, (h + 1) * DIL_HEAD_DIM)

    o_heads, ls = [], []
    slot = 0
    for gi, d in enumerate(dilations):
        if d == 1:
            o_heads.append(lambda h, rows, gi=gi: o_refs[gi][0, rows, head_slice(h)])
            ls.append(lambda rows, gi=gi: l_refs[gi][0, rows, :])
            continue
        for r in range(d):
            strided = pl.ds(r, tm // d, stride=d)
            lnat[slot, strided, :] = l_refs[gi][r]
            for h in range(DIL_HEADS):
                onat[slot * DIL_HEADS + h, strided, :] = o_refs[gi][r, :, head_slice(h)]
        o_heads.append(lambda h, rows, slot=slot: onat[slot * DIL_HEADS + h, rows, :])
        ls.append(lambda rows, slot=slot: lnat[slot, rows, :])
        slot += 1
    half = tm // 2
    for rows in (slice(0, half), slice(half, tm)):
        stats = [l(rows) for l in ls]
        dens = [pltpu.roll(st, LANES - DIL_HEADS, 1) for st in stats]
        mx = functools.reduce(jnp.maximum, stats)
        es = [jnp.exp(st - mx) for st in stats]
        tot = functools.reduce(lambda u, v: u + v, [dn * e for dn, e in zip(dens, es)])
        ws = [e / tot for e in es]
        cols = []
        for h in range(DIL_HEADS):
            c = ws[0][:, h:h + 1] * o_heads[0](h, rows)
            for gi in range(1, n_g):
                c = c + ws[gi][:, h:h + 1] * o_heads[gi](h, rows)
            cols.append(c.astype(BF16))
        a = jnp.concatenate(cols, axis=1)
        out_ref[rows, :] = h_ref[rows, :] + jnp.dot(a, w_ref[...], preferred_element_type=F32)


def _dil_combine(os_, lses, w_o, h, *, seq, tm):
    t, d_model = h.shape
    w = w_o.shape[0]
    dilations = tuple(o.shape[1] for o in os_)
    per_b = seq // tm
    n_strided = sum(1 for d in dilations if d > 1)

    def res_spec(d, width):
        return pl.BlockSpec((None, d, tm // d, width), lambda i: (i // per_b, 0, i % per_b, 0))

    row = lambda i: (i, 0)
    return pl.pallas_call(
        functools.partial(_dil_combine_kernel, dilations=dilations),
        grid=(t // tm,),
        in_specs=[res_spec(d, w) for d in dilations] + [res_spec(d, LANES) for d in dilations]
                 + [pl.BlockSpec((w, d_model), lambda i: (0, 0)), pl.BlockSpec((tm, d_model), row)],
        out_specs=pl.BlockSpec((tm, d_model), row),
        out_shape=jax.ShapeDtypeStruct((t, d_model), F32),
        scratch_shapes=[pltpu.VMEM((n_strided * DIL_HEADS, tm, DIL_HEAD_DIM), F32),
                        pltpu.VMEM((n_strided, tm, LANES), F32)],
        compiler_params=_cparams("parallel"),
        name="dil_combine",
    )(*os_, *lses, w_o, h)


def _mem_kv_kernel(mem_ref, g_ref, w_ref, o_ref):
    xn = _rms(mem_ref[...], g_ref[...]).astype(BF16)
    o_ref[...] = jnp.dot(xn, w_ref[...], preferred_element_type=F32).astype(o_ref.dtype)


def _mem_kv(mem, g, w_kv):
    b, m, d = mem.shape
    depth, _, n = w_kv.shape
    return pl.pallas_call(
        _mem_kv_kernel,
        grid=(depth, b),
        in_specs=[pl.BlockSpec((None, m, d), lambda l, bb: (bb, 0, 0)),
                  pl.BlockSpec((1, d), lambda l, bb: (0, 0)),
                  pl.BlockSpec((None, d, n), lambda l, bb: (l, 0, 0))],
        out_specs=pl.BlockSpec((None, None, m, n), lambda l, bb: (l, bb, 0, 0)),
        out_shape=jax.ShapeDtypeStruct((depth, b, m, n), BF16),
        compiler_params=_cparams("parallel", "parallel"),
        name="mem_kv",
    )(mem, g.reshape(1, d), w_kv)


def _mem_attn_kernel(h_ref, g_ref, wq_ref, kv_ref, wo_ref, o_ref):
    width = MEM_HEADS * MEM_HEAD_DIM
    scale = MEM_HEAD_DIM ** -0.5
    x = h_ref[...]
    xn = _rms(x, g_ref[...]).astype(BF16)
    q = jnp.dot(xn, wq_ref[...], preferred_element_type=F32)
    outs = []
    for hh in range(MEM_HEADS):
        sl = slice(hh * MEM_HEAD_DIM, (hh + 1) * MEM_HEAD_DIM)
        k = kv_ref[:, sl]
        v = kv_ref[:, width + hh * MEM_HEAD_DIM:width + (hh + 1) * MEM_HEAD_DIM]
        sc = lax.dot_general(q[:, sl].astype(BF16), k, (((1,), (1,)), ((), ())),
                             preferred_element_type=F32) * scale
        mx = jnp.max(sc, axis=-1, keepdims=True)
        p = jnp.exp(sc - mx)
        p = p / jnp.sum(p, axis=-1, keepdims=True)
        outs.append(jnp.dot(p.astype(BF16), v, preferred_element_type=F32).astype(BF16))
    a = jnp.concatenate(outs, axis=1)
    o_ref[...] = x + jnp.dot(a, wo_ref[...], preferred_element_type=F32)


def _mem_attn(h, g, w_q, kv, w_o, layer, *, batch, seq, tm):
    t, d = h.shape
    width = MEM_HEADS * MEM_HEAD_DIM
    m = kv.shape[2]
    per_b = seq // tm
    return pl.pallas_call(
        _mem_attn_kernel,
        grid=(t // tm,),
        in_specs=[pl.BlockSpec((tm, d), lambda i: (i, 0)),
                  pl.BlockSpec((1, d), lambda i: (0, 0)),
                  pl.BlockSpec((None, d, width), lambda i: (layer, 0, 0)),
                  pl.BlockSpec((None, None, m, 2 * width), lambda i: (layer, i // per_b, 0, 0)),
                  pl.BlockSpec((None, width, d), lambda i: (layer, 0, 0))],
        out_specs=pl.BlockSpec((tm, d), lambda i: (i, 0)),
        out_shape=jax.ShapeDtypeStruct((t, d), F32),
        compiler_params=_cparams("parallel"),
        name="mem_attn",
    )(h, g.reshape(1, d), w_q, kv, w_o)


def _ffn_kernel(h_ref, g_ref, wg_ref, wu_ref, wo_ref, gf_ref, o_ref, xn_ref, *, final_norm):
    f = pl.program_id(1)

    def body(first):
        for rows in _row_subtiles(h_ref.shape[0]):
            xn = _normed_rows(h_ref, g_ref, xn_ref, rows, first)
            gate = jnp.dot(xn, wg_ref[...], preferred_element_type=F32)
            up = jnp.dot(xn, wu_ref[...], preferred_element_type=F32)
            a = (_silu(gate) * up).astype(BF16)
            base = h_ref[rows, :] if first else o_ref[rows, :]
            o_ref[rows, :] = base + jnp.dot(a, wo_ref[...], preferred_element_type=F32)

    _first_step_variants(f, body)

    if final_norm:
        @pl.when(f == pl.num_programs(1) - 1)
        def _():
            o_ref[...] = _rms(o_ref[...], gf_ref[...])


def _ffn(h, g, w_in, w_out, layer, g_final, *, tm, tf, final_norm):
    t, d = h.shape
    d_ff = w_out.shape[1]
    nf = d_ff // tf
    return pl.pallas_call(
        functools.partial(_ffn_kernel, final_norm=final_norm),
        grid=(t // tm, nf),
        in_specs=[pl.BlockSpec((tm, d), lambda i, f: (i, 0)),
                  pl.BlockSpec((1, d), lambda i, f: (0, 0)),
                  pl.BlockSpec((None, d, tf), lambda i, f: (layer, 0, f)),
                  pl.BlockSpec((None, d, tf), lambda i, f: (layer, 0, nf + f)),
                  pl.BlockSpec((None, tf, d), lambda i, f: (layer, f, 0)),
                  pl.BlockSpec((1, d), lambda i, f: (0, 0))],
        out_specs=pl.BlockSpec((tm, d), lambda i, f: (i, 0)),
        out_shape=jax.ShapeDtypeStruct((t, d), F32),
        scratch_shapes=[pltpu.VMEM((tm, d), BF16)],
        compiler_params=_cparams("parallel", "arbitrary"),
        name="ffn",
    )(h, g.reshape(1, d), w_in, w_in, w_out, g_final.reshape(1, d))


def kernel(x, mem, positions, norm_mix, norm_mem, norm_ffn, norm_final, ssd_w_in, ssd_conv_w, ssd_conv_b, ssd_dt_bias, ssd_a_log, ssd_d, ssd_norm, ssd_w_out, kv_norm, w_kv_shared, dil_w_q, dil_w_o, mem_src_norm, mem_w_q, mem_w_kv, mem_w_o, ffn_w_in, ffn_w_out):
    batch, seq, d_model = x.shape
    depth = norm_mix.shape[0]
    n_a = ssd_w_in.shape[0]
    d_inner = ssd_w_out.shape[1]
    heads = d_inner // SSD_HEAD_DIM
    zx_dim = ssd_w_in.shape[2] - heads
    dil_w = DIL_HEADS * DIL_HEAD_DIM
    n_dil = len(DIL_PATTERNS)
    t = batch * seq

    h = x.reshape(t, d_model)
    rope = _rope_tables(positions)
    mem_kv = _mem_kv(mem, mem_src_norm, mem_w_kv.astype(BF16))
    mem_w_q16 = mem_w_q.astype(BF16)
    mem_w_o16 = mem_w_o.astype(BF16)
    ffn_w_in16 = ffn_w_in.astype(BF16)
    ffn_w_out16 = ffn_w_out.astype(BF16)
    ssd_w_in16 = ssd_w_in.astype(BF16)
    ssd_w_dt16 = jnp.pad(ssd_w_in[:, :, zx_dim:], ((0, 0), (0, 0), (0, LANES - heads))).astype(BF16)

    kv_sh = None
    for i in range(depth):
        if i < n_a:
            proj, dt_raw = _ssd_in_proj(h, norm_mix[i], ssd_w_in16, ssd_w_dt16, i, n_cols=zx_dim,
                                        tm=1024, tn=1024)
            y = _ssd_core(proj, dt_raw, ssd_conv_w[i], ssd_conv_b[i], ssd_dt_bias[i], ssd_a_log[i],
                          ssd_d[i], ssd_norm[i], seq=seq, d_inner=d_inner)
            h = _matmul_residual(y, ssd_w_out[i].astype(BF16), h, tm=512, tn=1024, name="ssd_out_proj")
        else:
            dils = tuple(d for _, d in DIL_PATTERNS)
            assert all(w // d == DIL_BLOCK for w, d in DIL_PATTERNS)
            proj = functools.partial(_norm_matmul_dil, batch=batch, seq=seq, tm=1024, tn=1024)
            if kv_sh is None:
                w_kv16 = w_kv_shared.astype(BF16)
                kv_sh = [proj(h, kv_norm, w_kv16, rope, col_block=c, dilation=dils[c % n_dil],
                              rope=c < n_dil, name=f"shared_kv_proj_{c}") for c in range(2 * n_dil)]
            j = i - n_a
            w_q16 = dil_w_q[j].astype(BF16)
            qs = [proj(h, norm_mix[i], w_q16, rope, col_block=c, dilation=dils[c], rope=True,
                       name=f"dil_q_proj_{c}") for c in range(n_dil)]
            outs, lses = [], []
            for g in range(n_dil):
                o, lse = _dil_attn(qs[g], kv_sh[g], kv_sh[n_dil + g])
                outs.append(o)
                lses.append(lse)
            h = _dil_combine(outs, lses, dil_w_o[j].astype(BF16), h, seq=seq, tm=256)
        h = _mem_attn(h, norm_mem[i], mem_w_q16, mem_kv, mem_w_o16, i, batch=batch, seq=seq, tm=512)
        h = _ffn(h, norm_ffn[i], ffn_w_in16, ffn_w_out16, i, norm_final, tm=1024, tf=512,
                 final_norm=(i == depth - 1))
    return h.reshape(batch, seq, d_model)
```

```python
import functools
import math

import jax
import jax.numpy as jnp
import numpy as np
from jax import lax
from jax.experimental import pallas as pl
from jax.experimental.pallas import tpu as pltpu

F32 = jnp.float32
BF16 = jnp.bfloat16

NORM_EPS = 1e-6
SSD_HEAD_DIM = 64
SSD_GROUPS = 8
SSD_STATE = 128
SSD_CONV = 4
SSD_CHUNK = 128
DIL_PATTERNS = ((128, 1), (512, 4), (2048, 16))
DIL_HEADS = 16
DIL_HEAD_DIM = 128
DIL_BLOCK = 128
ROPE_THETA = 10000.0
LOG2E = math.log2(math.e)
MEM_HEADS = 4
MEM_HEAD_DIM = 128

LANES = 128
SUBLANES = 8
VMEM_LIMIT = 56 * 1024 * 1024
ROW_SUB = 256
ATTN_BLOCKS_PER_STEP = 4


def _cparams(*sem):
    return pltpu.CompilerParams(dimension_semantics=sem, vmem_limit_bytes=VMEM_LIMIT)


def _rms(x, g):
    ms = jnp.mean(x * x, axis=-1, keepdims=True)
    return x * lax.rsqrt(ms + NORM_EPS) * g


def _silu(x):
    return x / (1.0 + jnp.exp(-x))


def _tile_major(w, tn):
    layers, k, n = w.shape
    return w.astype(BF16).reshape(layers, k, n // tn, tn).transpose(0, 2, 1, 3)


def _rope_table_kernel(pos_ref, freq_ref, cos_ref, sin_ref):
    ang = pos_ref[...].astype(F32) * freq_ref[...]
    lane = lax.broadcasted_iota(jnp.int32, ang.shape, 1)
    s = jnp.sin(ang)
    cos_ref[...] = jnp.cos(ang)
    sin_ref[...] = jnp.where(lane < DIL_HEAD_DIM // 2, -s, s)


def _rope_tables(positions):
    t = positions.size
    half = DIL_HEAD_DIM // 2
    inv_freq = ROPE_THETA ** (-jnp.arange(half, dtype=F32) / half)
    freq = jnp.concatenate([inv_freq, inv_freq]).reshape(1, DIL_HEAD_DIM)
    tm = 1024
    return pl.pallas_call(
        _rope_table_kernel,
        grid=(t // tm,),
        in_specs=[pl.BlockSpec((tm, 1), lambda i: (i, 0)),
                  pl.BlockSpec((1, DIL_HEAD_DIM), lambda i: (0, 0))],
        out_specs=[pl.BlockSpec((tm, DIL_HEAD_DIM), lambda i: (i, 0))] * 2,
        out_shape=[jax.ShapeDtypeStruct((t, DIL_HEAD_DIM), F32)] * 2,
        compiler_params=_cparams("parallel"),
        name="rope_tables",
    )(positions.reshape(t, 1), freq)


def _normed_rows(x_ref, g_ref, xn_ref, rows, first):
    if first:
        xn = _rms(x_ref[rows, :], g_ref[...]).astype(BF16)
        xn_ref[rows, :] = xn
        return xn
    return xn_ref[rows, :]


def _row_subtiles(tm):
    return [slice(k * ROW_SUB, (k + 1) * ROW_SUB) for k in range(tm // ROW_SUB)]


def _first_step_variants(step_id, body):
    pl.when(step_id == 0)(functools.partial(body, True))
    pl.when(step_id != 0)(functools.partial(body, False))


def _ssd_in_proj_kernel(x_ref, g_ref, w_ref, wdt_ref, o_ref, dt_ref, xn_ref):
    def body(first):
        for rows in _row_subtiles(x_ref.shape[0]):
            xn = _normed_rows(x_ref, g_ref, xn_ref, rows, first)
            o_ref[rows, :] = jnp.dot(xn, w_ref[...], preferred_element_type=F32)
            if first:
                dt_ref[rows, :] = jnp.dot(xn, wdt_ref[...], preferred_element_type=F32)

    _first_step_variants(pl.program_id(1), body)


def _ssd_in_proj(x, g, w, w_dt, layer, *, n_cols, tm, tn):
    t, d = x.shape
    return pl.pallas_call(
        _ssd_in_proj_kernel,
        grid=(t // tm, n_cols // tn),
        in_specs=[pl.BlockSpec((tm, d), lambda i, j: (i, 0)),
                  pl.BlockSpec((1, d), lambda i, j: (0, 0)),
                  pl.BlockSpec((None, d, tn), lambda i, j: (layer, 0, j)),
                  pl.BlockSpec((None, d, LANES), lambda i, j: (layer, 0, 0))],
        out_specs=[pl.BlockSpec((tm, tn), lambda i, j: (i, j)),
                   pl.BlockSpec((tm, LANES), lambda i, j: (i, 0))],
        out_shape=[jax.ShapeDtypeStruct((t, n_cols), F32), jax.ShapeDtypeStruct((t, LANES), F32)],
        scratch_shapes=[pltpu.VMEM((tm, d), BF16)],
        compiler_params=_cparams("parallel", "arbitrary"),
        name="ssd_in_proj",
    )(x, g.reshape(1, d), w, w_dt)


def _residue_permutation(d):
    per_res = ROW_SUB // d
    out_row = lax.broadcasted_iota(jnp.int32, (ROW_SUB, ROW_SUB), 0)
    in_row = lax.broadcasted_iota(jnp.int32, (ROW_SUB, ROW_SUB), 1)
    src = jnp.bitwise_and(out_row, per_res - 1) * d + jnp.right_shift(out_row, per_res.bit_length() - 1)
    return jnp.where(in_row == src, 1.0, 0.0).astype(BF16)


def _norm_matmul_dil_kernel(x_ref, g_ref, w_ref, cos_ref, sin_ref, o_ref, xn_ref, acc_ref, *, dilation, rope):
    d = dilation
    tn = w_ref.shape[1]
    per_res = ROW_SUB // d
    permute_rows = d > 1 and per_res <= 2 * SUBLANES

    def residue_major(ref, k):
        return jnp.concatenate([ref[pl.ds(k * ROW_SUB + r, per_res, stride=d), :] for r in range(d)], axis=0)

    def body(first):
        perm = _residue_permutation(d) if (first and permute_rows) else None
        for k, rows in enumerate(_row_subtiles(x_ref.shape[0])):
            if first and permute_rows:
                xn = _rms(x_ref[rows, :], g_ref[...]).astype(BF16)
                xn = jnp.dot(perm, xn, preferred_element_type=F32).astype(BF16)
                xn_ref[rows, :] = xn
            else:
                xn = _normed_rows(x_ref, g_ref, xn_ref, rows, first)
            acc = jnp.dot(xn, w_ref[...], preferred_element_type=F32)
            if rope:
                cos = residue_major(cos_ref, k) if permute_rows else cos_ref[rows, :]
                sin = residue_major(sin_ref, k) if permute_rows else sin_ref[rows, :]
            for hh in range(tn // DIL_HEAD_DIM):
                sl = slice(hh * DIL_HEAD_DIM, (hh + 1) * DIL_HEAD_DIM)
                t = acc[:, sl]
                if rope:
                    t = t * cos + pltpu.roll(t, DIL_HEAD_DIM // 2, 1) * sin
                if d == 1:
                    o_ref[0, rows, sl] = t.astype(BF16)
                elif permute_rows:
                    t16 = t.astype(BF16)
                    for r in range(d):
                        o_ref[r, k * per_res:(k + 1) * per_res, sl] = t16[r * per_res:(r + 1) * per_res, :]
                else:
                    acc_ref[k, hh] = t
                    for r in range(d):
                        o_ref[r, k * per_res:(k + 1) * per_res, sl] = (
                            acc_ref[k, hh, pl.ds(r, per_res, stride=d), :].astype(BF16))

    _first_step_variants(pl.program_id(1), body)


def _norm_matmul_dil(x, g, w, layer, rope_tables, *, col_block, dilation, rope, batch, seq, tm, name):
    t, d_model = x.shape
    tn = w.shape[-1]
    width = DIL_HEADS * DIL_HEAD_DIM
    tiles = width // tn
    per_b = seq // tm
    d = dilation
    return pl.pallas_call(
        functools.partial(_norm_matmul_dil_kernel, dilation=d, rope=rope),
        grid=(t // tm, tiles),
        in_specs=[pl.BlockSpec((tm, d_model), lambda i, j: (i, 0)),
                  pl.BlockSpec((1, d_model), lambda i, j: (0, 0)),
                  pl.BlockSpec((None, None, d_model, tn), lambda i, j: (layer, col_block * tiles + j, 0, 0)),
                  pl.BlockSpec((tm, DIL_HEAD_DIM), lambda i, j: (i, 0)),
                  pl.BlockSpec((tm, DIL_HEAD_DIM), lambda i, j: (i, 0))],
        out_specs=pl.BlockSpec((None, d, tm // d, tn), lambda i, j: (i // per_b, 0, i % per_b, j)),
        out_shape=jax.ShapeDtypeStruct((batch, d, seq // d, width), BF16),
        scratch_shapes=[pltpu.VMEM((tm, d_model), BF16),
                        pltpu.VMEM((tm // ROW_SUB, tn // DIL_HEAD_DIM, ROW_SUB, DIL_HEAD_DIM), F32)],
        compiler_params=_cparams("parallel", "arbitrary"),
        name=name,
    )(x, g.reshape(1, d_model), w, *rope_tables)


def _matmul_residual_kernel(a_ref, w_ref, h_ref, o_ref):
    o_ref[...] = h_ref[...] + jnp.dot(a_ref[...], w_ref[...], preferred_element_type=F32)


def _matmul_residual(a, w, layer, h, *, tm, tn, name="matmul_residual"):
    t, k = a.shape
    n = w.shape[2]
    return pl.pallas_call(
        _matmul_residual_kernel,
        grid=(n // tn, t // tm),
        in_specs=[pl.BlockSpec((tm, k), lambda j, i: (i, 0)),
                  pl.BlockSpec((None, k, tn), lambda j, i: (layer, 0, j)),
                  pl.BlockSpec((tm, tn), lambda j, i: (i, j))],
        out_specs=pl.BlockSpec((tm, tn), lambda j, i: (i, j)),
        out_shape=jax.ShapeDtypeStruct((t, n), F32),
        compiler_params=_cparams("parallel", "parallel"),
        name=name,
    )(a, w, h)


def _ssd_kernel(z_ref, xs_ref, b_ref, c_ref, dt_ref, convw_ref, convb_ref, dtb_ref, aneg_ref,
                dexp_ref, nw_ref, o_ref, xbuf, xc, state, *, chunks_per_seq):
    L = SSD_CHUNK
    d_inner = xs_ref.shape[1]
    bc_dim = b_ref.shape[1]
    conv_dim = d_inner + 2 * bc_dim
    gw = d_inner // SSD_GROUPS
    halo = SUBLANES
    first = pl.program_id(0) % chunks_per_seq == 0

    @pl.when(first)
    def _():
        xbuf[:, 0:halo, :] = jnp.zeros((xbuf.shape[0], halo, LANES), F32)
        state[...] = jnp.zeros(state.shape, F32)

    @pl.when(jnp.logical_not(first))
    def _():
        xbuf[:, 0:halo, :] = xbuf[:, L:L + halo, :]

    n_cb = conv_dim // LANES
    for cb in range(n_cb):
        c0 = cb * LANES
        if c0 < d_inner:
            src = xs_ref[:, c0:c0 + LANES]
        elif c0 < d_inner + bc_dim:
            src = b_ref[:, c0 - d_inner:c0 - d_inner + LANES]
        else:
            src = c_ref[:, c0 - d_inner - bc_dim:c0 - d_inner - bc_dim + LANES]
        xbuf[cb, halo:halo + L, :] = src

    nres = SUBLANES
    rows = L // nres
    for cb in range(n_cb):
        sl = slice(cb * LANES, (cb + 1) * LANES)
        taps = [convw_ref[k:k + 1, sl] for k in range(SSD_CONV)]
        bias = convb_ref[:, sl]
        cur = [xbuf[cb, pl.ds(halo + r, rows, stride=nres), :] for r in range(nres)]
        prv = {r: xbuf[cb, pl.ds(r, rows, stride=nres), :]
               for r in range(nres - SSD_CONV + 1, nres)}
        for r in range(nres):
            acc = bias
            for k in range(SSD_CONV):
                q = r - (SSD_CONV - 1) + k
                acc = acc + taps[k] * (cur[q] if q >= 0 else prv[q + nres])
            xc[cb, pl.ds(r, rows, stride=nres), :] = _silu(acc)

    raw = dt_ref[...] + dtb_ref[...]
    dt = jnp.maximum(raw, 0.0) + jnp.log1p(jnp.exp(-jnp.abs(raw)))
    acum = dt * aneg_ref[...]
    row = lax.broadcasted_iota(jnp.int32, (L, LANES), 0)
    sh = 1
    while sh < L:
        acum = acum + jnp.where(row >= sh, pltpu.roll(acum, sh, 0), 0.0)
        sh *= 2
    acum2 = acum * LOG2E
    acum2_t = acum2.T
    dt_t = dt.T
    ea = jnp.exp(acum)
    wend = jnp.exp(acum[L - 1:L, :] - acum) * dt

    ti = lax.broadcasted_iota(jnp.int32, (L, L), 0)
    si = lax.broadcasted_iota(jnp.int32, (L, L), 1)
    causal = ti >= si
    lo_half = si < SSD_HEAD_DIM

    def pair_expand(v, h0):
        return jnp.where(lo_half, v[:, h0:h0 + 1], v[:, h0 + 1:h0 + 2])

    heads_per_group = gw // SSD_HEAD_DIM
    for g in range(SSD_GROUPS):
        bg = xc[(d_inner + g * SSD_STATE) // LANES]
        cg = xc[(d_inner + bc_dim + g * SSD_STATE) // LANES]
        cb16 = cg.astype(BF16)
        cb = lax.dot_general(cb16, bg.astype(BF16), (((1,), (1,)), ((), ())),
                             preferred_element_type=F32)
        bt16 = bg.T.astype(BF16)
        st_old = state[g]
        yoff = jnp.dot(cb16, st_old.astype(BF16), preferred_element_type=F32)
        ys, xws, cds = [], [], []
        for pp in range(heads_per_group // 2):
            h0 = g * heads_per_group + 2 * pp
            c0 = g * gw + pp * LANES
            xpair = xc[c0 // LANES]
            xp16 = xpair.astype(BF16)
            yd = []
            for hh in (h0, h0 + 1):
                seg = acum2[:, hh:hh + 1] - acum2_t[hh:hh + 1, :]
                dec = jnp.exp2(jnp.where(causal, seg, -jnp.inf))
                m = (cb * dec * dt_t[hh:hh + 1, :]).astype(BF16)
                yd.append(jnp.dot(m, xp16, preferred_element_type=F32))
            ea_pair = pair_expand(ea, h0)
            y = (jnp.where(lo_half, yd[0], yd[1])
                 + yoff[:, pp * LANES:(pp + 1) * LANES] * ea_pair
                 + xpair * dexp_ref[:, c0:c0 + LANES])
            y = y * _silu(z_ref[:, c0:c0 + LANES])
            ys.append(y)
            xws.append((xpair * pair_expand(wend, h0)).astype(BF16))
            cds.append(ea_pair[L - 1:L, :])
        xw = jnp.concatenate(xws, axis=1)
        cd = jnp.concatenate(cds, axis=1)
        state[g] = st_old * cd + jnp.dot(bt16, xw, preferred_element_type=F32)
        yg = jnp.concatenate(ys, axis=1)
        ms = jnp.mean(yg * yg, axis=-1, keepdims=True)
        yn = yg * lax.rsqrt(ms + NORM_EPS) * nw_ref[:, g * gw:(g + 1) * gw]
        o_ref[:, g * gw:(g + 1) * gw] = yn.astype(o_ref.dtype)


def _ssd_core(proj, dt_raw, conv_w, conv_b, dt_bias, a_log, d_skip, norm_w, *, seq, d_inner):
    t = proj.shape[0]
    L = SSD_CHUNK
    bc_dim = SSD_GROUPS * SSD_STATE
    conv_dim = d_inner + 2 * bc_dim
    heads = d_inner // SSD_HEAD_DIM
    pad = LANES - heads

    def padrow(v):
        return jnp.pad(v.astype(F32), (0, pad)).reshape(1, LANES)

    aneg = padrow(-jnp.exp(a_log.astype(F32)))
    dexp = jnp.repeat(d_skip.astype(F32), SSD_HEAD_DIM).reshape(1, d_inner)
    xs_blk = d_inner // d_inner
    b_blk = (2 * d_inner) // bc_dim
    c_blk = b_blk + 1
    const = lambda c: (0, 0)
    return pl.pallas_call(
        functools.partial(_ssd_kernel, chunks_per_seq=seq // L),
        grid=(t // L,),
        in_specs=[pl.BlockSpec((L, d_inner), lambda c: (c, 0)),
                  pl.BlockSpec((L, d_inner), lambda c: (c, xs_blk)),
                  pl.BlockSpec((L, bc_dim), lambda c: (c, b_blk)),
                  pl.BlockSpec((L, bc_dim), lambda c: (c, c_blk)),
                  pl.BlockSpec((L, LANES), lambda c: (c, 0)),
                  pl.BlockSpec((SSD_CONV, conv_dim), const),
                  pl.BlockSpec((1, conv_dim), const),
                  pl.BlockSpec((1, LANES), const),
                  pl.BlockSpec((1, LANES), const),
                  pl.BlockSpec((1, d_inner), const),
                  pl.BlockSpec((1, d_inner), const)],
        out_specs=pl.BlockSpec((L, d_inner), lambda c: (c, 0)),
        out_shape=jax.ShapeDtypeStruct((t, d_inner), BF16),
        scratch_shapes=[pltpu.VMEM((conv_dim // LANES, L + 2 * SUBLANES, LANES), F32),
                        pltpu.VMEM((conv_dim // LANES, L, LANES), F32),
                        pltpu.VMEM((SSD_GROUPS, SSD_STATE, d_inner // SSD_GROUPS), F32)],
        compiler_params=_cparams("arbitrary"),
        name="ssd_core",
    )(proj, proj, proj, proj, dt_raw, conv_w.astype(F32), conv_b.reshape(1, conv_dim).astype(F32),
      padrow(dt_bias), aneg, dexp, norm_w.reshape(1, d_inner).astype(F32))


def _dil_attn_kernel(q_ref, k_ref, v_ref, o_ref, st_ref, kprev, vprev, bias_ref):
    blk = DIL_BLOCK
    n_blk = q_ref.shape[0] // blk
    i = pl.program_id(1)

    @pl.when(i == 0)
    def _():
        kprev[...] = jnp.zeros(kprev.shape, kprev.dtype)
        vprev[...] = jnp.zeros(vprev.shape, vprev.dtype)

    qi = lax.broadcasted_iota(jnp.int32, (blk, 2 * blk), 0)
    ki = lax.broadcasted_iota(jnp.int32, (blk, 2 * blk), 1)
    dist = qi + blk - ki
    band = (dist >= 0) & (dist <= blk)
    first_key = jnp.where(i > 0, 0, blk)
    bias_ref[0] = jnp.where(band & (ki >= first_key), 0.0, -jnp.inf)
    bias_ref[1] = jnp.where(band, 0.0, -jnp.inf)
    st_ref[...] = jnp.zeros(st_ref.shape, F32)
    scale = DIL_HEAD_DIM ** -0.5
    for j in range(n_blk):
        rows = slice(j * blk, (j + 1) * blk)
        prev_rows = slice((j - 1) * blk, j * blk)
        for h in range(DIL_HEADS):
            sl = slice(h * DIL_HEAD_DIM, (h + 1) * DIL_HEAD_DIM)
            k_prev = kprev[:, sl] if j == 0 else k_ref[prev_rows, sl]
            v_prev = vprev[:, sl] if j == 0 else v_ref[prev_rows, sl]
            k = jnp.concatenate([k_prev, k_ref[rows, sl]], axis=0)
            v = jnp.concatenate([v_prev, v_ref[rows, sl]], axis=0)
            raw = lax.dot_general(q_ref[rows, sl], k, (((1,), (1,)), ((), ())), preferred_element_type=F32)
            raw = raw + bias_ref[min(j, 1)]
            mx = jnp.max(raw, axis=-1, keepdims=True)
            p = jnp.exp2((raw - mx) * (scale * LOG2E))
            o_ref[rows, sl] = jnp.dot(p.astype(BF16), v, preferred_element_type=F32)
            st_ref[rows, h:h + 1] = mx * scale
            st_ref[rows, DIL_HEADS + h:DIL_HEADS + h + 1] = jnp.sum(p, axis=-1, keepdims=True)
    last = slice((n_blk - 1) * blk, n_blk * blk)
    kprev[...] = k_ref[last, :]
    vprev[...] = v_ref[last, :]


def _dil_attn(q, k, v):
    b, d, n, w = q.shape
    seqs = b * d
    rows = min(n, ATTN_BLOCKS_PER_STEP * DIL_BLOCK)
    blk = pl.BlockSpec((None, rows, w), lambda s, i: (s, i, 0))
    o, st = pl.pallas_call(
        _dil_attn_kernel,
        grid=(seqs, n // rows),
        in_specs=[blk, blk, blk],
        out_specs=[blk, pl.BlockSpec((None, rows, LANES), lambda s, i: (s, i, 0))],
        out_shape=[jax.ShapeDtypeStruct((seqs, n, w), F32),
                   jax.ShapeDtypeStruct((seqs, n, LANES), F32)],
        scratch_shapes=[pltpu.VMEM((DIL_BLOCK, w), BF16), pltpu.VMEM((DIL_BLOCK, w), BF16),
                        pltpu.VMEM((2, DIL_BLOCK, 2 * DIL_BLOCK), F32)],
        compiler_params=_cparams("arbitrary", "arbitrary"),
        name=f"dil_attn_d{d}",
    )(q.reshape(seqs, n, w), k.reshape(seqs, n, w), v.reshape(seqs, n, w))
    return o.reshape(b, d, n, w), st.reshape(b, d, n, LANES)


def _dil_combine_kernel(*refs, dilations):
    n_g = len(dilations)
    o_refs, l_refs = refs[:n_g], refs[n_g:2 * n_g]
    w_ref, h_ref, out_ref, onat, lnat = refs[2 * n_g:]
    tm = h_ref.shape[0]

    def head_slice(h):
        return slice(h * DIL_HEAD_DIM, (h + 1) * DIL_HEAD_DIM)

    o_heads, ls = [], []
    slot = 0
    for gi, d in enumerate(dilations):
        if d == 1:
            o_heads.append(lambda h, rows, gi=gi: o_refs[gi][0, rows, head_slice(h)])
            ls.append(lambda rows, gi=gi: l_refs[gi][0, rows, :])
            continue
        for r in range(d):
            strided = pl.ds(r, tm // d, stride=d)
            lnat[slot, strided, :] = l_refs[gi][r]
            for h in range(DIL_HEADS):
                onat[slot * DIL_HEADS + h, strided, :] = o_refs[gi][r, :, head_slice(h)]
        o_heads.append(lambda h, rows, slot=slot: onat[slot * DIL_HEADS + h, rows, :])
        ls.append(lambda rows, slot=slot: lnat[slot, rows, :])
        slot += 1
    half = tm // 2
    for rows in (slice(0, half), slice(half, tm)):
        stats = [l(rows) for l in ls]
        dens = [pltpu.roll(st, LANES - DIL_HEADS, 1) for st in stats]
        mx = functools.reduce(jnp.maximum, stats)
        es = [jnp.exp(st - mx) for st in stats]
        tot = functools.reduce(lambda u, v: u + v, [dn * e for dn, e in zip(dens, es)])
        ws = [e / tot for e in es]
        cols = []
        for h in range(DIL_HEADS):
            c = ws[0][:, h:h + 1] * o_heads[0](h, rows)
            for gi in range(1, n_g):
                c = c + ws[gi][:, h:h + 1] * o_heads[gi](h, rows)
            cols.append(c.astype(BF16))
        a = jnp.concatenate(cols, axis=1)
        out_ref[rows, :] = h_ref[rows, :] + jnp.dot(a, w_ref[...], preferred_element_type=F32)


def _dil_combine(os_, lses, w_o, layer, h, *, seq, tm):
    t, d_model = h.shape
    w = w_o.shape[1]
    dilations = tuple(o.shape[1] for o in os_)
    per_b = seq // tm
    n_strided = sum(1 for d in dilations if d > 1)

    def res_spec(d, width):
        return pl.BlockSpec((None, d, tm // d, width), lambda i: (i // per_b, 0, i % per_b, 0))

    row = lambda i: (i, 0)
    return pl.pallas_call(
        functools.partial(_dil_combine_kernel, dilations=dilations),
        grid=(t // tm,),
        in_specs=[res_spec(d, w) for d in dilations] + [res_spec(d, LANES) for d in dilations]
                 + [pl.BlockSpec((None, w, d_model), lambda i: (layer, 0, 0)), pl.BlockSpec((tm, d_model), row)],
        out_specs=pl.BlockSpec((tm, d_model), row),
        out_shape=jax.ShapeDtypeStruct((t, d_model), F32),
        scratch_shapes=[pltpu.VMEM((n_strided * DIL_HEADS, tm, DIL_HEAD_DIM), F32),
                        pltpu.VMEM((n_strided, tm, LANES), F32)],
        compiler_params=_cparams("parallel"),
        name="dil_combine",
    )(*os_, *lses, w_o, h)


def _mem_kv_kernel(mem_ref, g_ref, w_ref, o_ref):
    xn = _rms(mem_ref[...], g_ref[...]).astype(BF16)
    o_ref[...] = jnp.dot(xn, w_ref[...], preferred_element_type=F32).astype(o_ref.dtype)


def _mem_kv(mem, g, w_kv):
    b, m, d = mem.shape
    depth, _, n = w_kv.shape
    return pl.pallas_call(
        _mem_kv_kernel,
        grid=(depth, b),
        in_specs=[pl.BlockSpec((None, m, d), lambda l, bb: (bb, 0, 0)),
                  pl.BlockSpec((1, d), lambda l, bb: (0, 0)),
                  pl.BlockSpec((None, d, n), lambda l, bb: (l, 0, 0))],
        out_specs=pl.BlockSpec((None, None, m, n), lambda l, bb: (l, bb, 0, 0)),
        out_shape=jax.ShapeDtypeStruct((depth, b, m, n), BF16),
        compiler_params=_cparams("parallel", "parallel"),
        name="mem_kv",
    )(mem, g.reshape(1, d), w_kv)


def _mem_attn_kernel(h_ref, g_ref, wq_ref, kv_ref, wo_ref, o_ref):
    width = MEM_HEADS * MEM_HEAD_DIM
    scale = MEM_HEAD_DIM ** -0.5
    x = h_ref[...]
    xn = _rms(x, g_ref[...]).astype(BF16)
    q = jnp.dot(xn, wq_ref[...], preferred_element_type=F32)
    outs = []
    for hh in range(MEM_HEADS):
        sl = slice(hh * MEM_HEAD_DIM, (hh + 1) * MEM_HEAD_DIM)
        k = kv_ref[:, sl]
        v = kv_ref[:, width + hh * MEM_HEAD_DIM:width + (hh + 1) * MEM_HEAD_DIM]
        sc = lax.dot_general(q[:, sl].astype(BF16), k, (((1,), (1,)), ((), ())),
                             preferred_element_type=F32) * scale
        mx = jnp.max(sc, axis=-1, keepdims=True)
        p = jnp.exp(sc - mx)
        p = p / jnp.sum(p, axis=-1, keepdims=True)
        outs.append(jnp.dot(p.astype(BF16), v, preferred_element_type=F32).astype(BF16))
    a = jnp.concatenate(outs, axis=1)
    o_ref[...] = x + jnp.dot(a, wo_ref[...], preferred_element_type=F32)


def _mem_attn(h, g, w_q, kv, w_o, layer, *, batch, seq, tm):
    t, d = h.shape
    width = MEM_HEADS * MEM_HEAD_DIM
    m = kv.shape[2]
    per_b = seq // tm
    return pl.pallas_call(
        _mem_attn_kernel,
        grid=(t // tm,),
        in_specs=[pl.BlockSpec((tm, d), lambda i: (i, 0)),
                  pl.BlockSpec((1, d), lambda i: (0, 0)),
                  pl.BlockSpec((None, d, width), lambda i: (layer, 0, 0)),
                  pl.BlockSpec((None, None, m, 2 * width), lambda i: (layer, i // per_b, 0, 0)),
                  pl.BlockSpec((None, width, d), lambda i: (layer, 0, 0))],
        out_specs=pl.BlockSpec((tm, d), lambda i: (i, 0)),
        out_shape=jax.ShapeDtypeStruct((t, d), F32),
        compiler_params=_cparams("parallel"),
        name="mem_attn",
    )(h, g.reshape(1, d), w_q, kv, w_o)


def _ffn_kernel(h_ref, g_ref, wg_ref, wu_ref, wo_ref, gf_ref, o_ref, xn_ref, *, final_norm):
    f = pl.program_id(1)

    def body(first):
        for rows in _row_subtiles(h_ref.shape[0]):
            xn = _normed_rows(h_ref, g_ref, xn_ref, rows, first)
            gate = jnp.dot(xn, wg_ref[...], preferred_element_type=F32)
            up = jnp.dot(xn, wu_ref[...], preferred_element_type=F32)
            a = (_silu(gate) * up).astype(BF16)
            base = h_ref[rows, :] if first else o_ref[rows, :]
            o_ref[rows, :] = base + jnp.dot(a, wo_ref[...], preferred_element_type=F32)

    _first_step_variants(f, body)

    if final_norm:
        @pl.when(f == pl.num_programs(1) - 1)
        def _():
            o_ref[...] = _rms(o_ref[...], gf_ref[...])


def _ffn(h, g, w_in, w_out, layer, g_final, *, tm, final_norm):
    t, d = h.shape
    tf = w_in.shape[-1]
    d_ff = w_out.shape[1]
    nf = d_ff // tf
    return pl.pallas_call(
        functools.partial(_ffn_kernel, final_norm=final_norm),
        grid=(t // tm, nf),
        in_specs=[pl.BlockSpec((tm, d), lambda i, f: (i, 0)),
                  pl.BlockSpec((1, d), lambda i, f: (0, 0)),
                  pl.BlockSpec((None, None, d, tf), lambda i, f: (layer, f, 0, 0)),
                  pl.BlockSpec((None, None, d, tf), lambda i, f: (layer, nf + f, 0, 0)),
                  pl.BlockSpec((None, tf, d), lambda i, f: (layer, f, 0)),
                  pl.BlockSpec((1, d), lambda i, f: (0, 0))],
        out_specs=pl.BlockSpec((tm, d), lambda i, f: (i, 0)),
        out_shape=jax.ShapeDtypeStruct((t, d), F32),
        scratch_shapes=[pltpu.VMEM((tm, d), BF16)],
        compiler_params=_cparams("parallel", "arbitrary"),
        name="ffn",
    )(h, g.reshape(1, d), w_in, w_in, w_out, g_final.reshape(1, d))


def kernel(x, mem, positions, norm_mix, norm_mem, norm_ffn, norm_final, ssd_w_in, ssd_conv_w, ssd_conv_b, ssd_dt_bias, ssd_a_log, ssd_d, ssd_norm, ssd_w_out, kv_norm, w_kv_shared, dil_w_q, dil_w_o, mem_src_norm, mem_w_q, mem_w_kv, mem_w_o, ffn_w_in, ffn_w_out):
    batch, seq, d_model = x.shape
    depth = norm_mix.shape[0]
    n_a = ssd_w_in.shape[0]
    d_inner = ssd_w_out.shape[1]
    heads = d_inner // SSD_HEAD_DIM
    zx_dim = ssd_w_in.shape[2] - heads
    dil_w = DIL_HEADS * DIL_HEAD_DIM
    n_dil = len(DIL_PATTERNS)
    t = batch * seq

    h = x.reshape(t, d_model)
    rope = _rope_tables(positions)
    mem_kv = _mem_kv(mem, mem_src_norm, mem_w_kv.astype(BF16))
    mem_w_q16 = mem_w_q.astype(BF16)
    mem_w_o16 = mem_w_o.astype(BF16)
    ffn_w_in16 = _tile_major(ffn_w_in, 512)
    ffn_w_out16 = ffn_w_out.astype(BF16)
    ssd_w_in16 = ssd_w_in.astype(BF16)
    ssd_w_out16 = ssd_w_out.astype(BF16)
    w_kv16 = _tile_major(w_kv_shared[None], 1024)
    dil_w_q16 = _tile_major(dil_w_q, 1024)
    dil_w_o16 = dil_w_o.astype(BF16)
    ssd_w_dt16 = jnp.pad(ssd_w_in[:, :, zx_dim:], ((0, 0), (0, 0), (0, LANES - heads))).astype(BF16)

    kv_sh = None
    for i in range(depth):
        if i < n_a:
            proj, dt_raw = _ssd_in_proj(h, norm_mix[i], ssd_w_in16, ssd_w_dt16, i, n_cols=zx_dim,
                                        tm=1024, tn=1024)
            y = _ssd_core(proj, dt_raw, ssd_conv_w[i], ssd_conv_b[i], ssd_dt_bias[i], ssd_a_log[i],
                          ssd_d[i], ssd_norm[i], seq=seq, d_inner=d_inner)
            h = _matmul_residual(y, ssd_w_out16, i, h, tm=512, tn=1024, name="ssd_out_proj")
        else:
            dils = tuple(d for _, d in DIL_PATTERNS)
            assert all(w // d == DIL_BLOCK for w, d in DIL_PATTERNS)
            proj = functools.partial(_norm_matmul_dil, batch=batch, seq=seq, tm=1024)
            if kv_sh is None:
                kv_sh = [proj(h, kv_norm, w_kv16, 0, rope, col_block=c, dilation=dils[c % n_dil],
                              rope=c < n_dil, name=f"shared_kv_proj_{c}") for c in range(2 * n_dil)]
            j = i - n_a
            qs = [proj(h, norm_mix[i], dil_w_q16, j, rope, col_block=c, dilation=dils[c], rope=True,
                       name=f"dil_q_proj_{c}") for c in range(n_dil)]
            outs, lses = [], []
            for g in range(n_dil):
                o, lse = _dil_attn(qs[g], kv_sh[g], kv_sh[n_dil + g])
                outs.append(o)
                lses.append(lse)
            h = _dil_combine(outs, lses, dil_w_o16, j, h, seq=seq, tm=256)
        h = _mem_attn(h, norm_mem[i], mem_w_q16, mem_kv, mem_w_o16, i, batch=batch, seq=seq, tm=512)
        h = _ffn(h, norm_ffn[i], ffn_w_in16, ffn_w_out16, i, norm_final, tm=1024,
                 final_norm=(i == depth - 1))
    return h.reshape(batch, seq, d_model)
```

```python
import functools
import math

import jax
import jax.numpy as jnp
import numpy as np
from jax import lax
from jax.experimental import pallas as pl
from jax.experimental.pallas import tpu as pltpu

F32 = jnp.float32
BF16 = jnp.bfloat16

NORM_EPS = 1e-6
SSD_HEAD_DIM = 64
SSD_GROUPS = 8
SSD_STATE = 128
SSD_CONV = 4
SSD_CHUNK = 128
DIL_PATTERNS = ((128, 1), (512, 4), (2048, 16))
DIL_HEADS = 16
DIL_HEAD_DIM = 128
DIL_BLOCK = 128
ROPE_THETA = 10000.0
LOG2E = math.log2(math.e)
MEM_HEADS = 4
MEM_HEAD_DIM = 128

LANES = 128
SUBLANES = 8
VMEM_LIMIT = 56 * 1024 * 1024
ROW_SUB = 256
ATTN_BLOCKS_PER_STEP = 4


def _cparams(*sem):
    return pltpu.CompilerParams(dimension_semantics=sem, vmem_limit_bytes=VMEM_LIMIT)


def _rms(x, g):
    ms = jnp.mean(x * x, axis=-1, keepdims=True)
    return x * lax.rsqrt(ms + NORM_EPS) * g


def _silu(x):
    return x / (1.0 + jnp.exp(-x))


def _rope_table_kernel(pos_ref, freq_ref, cos_ref, sin_ref):
    ang = pos_ref[...].astype(F32) * freq_ref[...]
    lane = lax.broadcasted_iota(jnp.int32, ang.shape, 1)
    s = jnp.sin(ang)
    cos_ref[...] = jnp.cos(ang)
    sin_ref[...] = jnp.where(lane < DIL_HEAD_DIM // 2, -s, s)


def _rope_tables(positions):
    t = positions.size
    half = DIL_HEAD_DIM // 2
    inv_freq = ROPE_THETA ** (-jnp.arange(half, dtype=F32) / half)
    freq = jnp.concatenate([inv_freq, inv_freq]).reshape(1, DIL_HEAD_DIM)
    tm = 1024
    return pl.pallas_call(
        _rope_table_kernel,
        grid=(t // tm,),
        in_specs=[pl.BlockSpec((tm, 1), lambda i: (i, 0)),
                  pl.BlockSpec((1, DIL_HEAD_DIM), lambda i: (0, 0))],
        out_specs=[pl.BlockSpec((tm, DIL_HEAD_DIM), lambda i: (i, 0))] * 2,
        out_shape=[jax.ShapeDtypeStruct((t, DIL_HEAD_DIM), F32)] * 2,
        compiler_params=_cparams("parallel"),
        name="rope_tables",
    )(positions.reshape(t, 1), freq)


def _normed_rows(x_ref, g_ref, xn_ref, rows, first):
    if first:
        xn = _rms(x_ref[rows, :], g_ref[...]).astype(BF16)
        xn_ref[rows, :] = xn
        return xn
    return xn_ref[rows, :]


def _row_subtiles(tm):
    return [slice(k * ROW_SUB, (k + 1) * ROW_SUB) for k in range(tm // ROW_SUB)]


def _first_step_variants(step_id, body):
    pl.when(step_id == 0)(functools.partial(body, True))
    pl.when(step_id != 0)(functools.partial(body, False))


def _ssd_in_proj_kernel(x_ref, g_ref, w_ref, wdt_ref, o_ref, dt_ref, xn_ref):
    def body(first):
        for rows in _row_subtiles(x_ref.shape[0]):
            xn = _normed_rows(x_ref, g_ref, xn_ref, rows, first)
            o_ref[rows, :] = jnp.dot(xn, w_ref[...], preferred_element_type=F32)
            if first:
                dt_ref[rows, :] = jnp.dot(xn, wdt_ref[...], preferred_element_type=F32)

    _first_step_variants(pl.program_id(1), body)


def _ssd_in_proj(x, g, w, w_dt, layer, *, n_cols, tm, tn):
    t, d = x.shape
    return pl.pallas_call(
        _ssd_in_proj_kernel,
        grid=(t // tm, n_cols // tn),
        in_specs=[pl.BlockSpec((tm, d), lambda i, j: (i, 0)),
                  pl.BlockSpec((1, d), lambda i, j: (0, 0)),
                  pl.BlockSpec((None, d, tn), lambda i, j: (layer, 0, j)),
                  pl.BlockSpec((None, d, LANES), lambda i, j: (layer, 0, 0))],
        out_specs=[pl.BlockSpec((tm, tn), lambda i, j: (i, j)),
                   pl.BlockSpec((tm, LANES), lambda i, j: (i, 0))],
        out_shape=[jax.ShapeDtypeStruct((t, n_cols), F32), jax.ShapeDtypeStruct((t, LANES), F32)],
        scratch_shapes=[pltpu.VMEM((tm, d), BF16)],
        compiler_params=_cparams("parallel", "arbitrary"),
        name="ssd_in_proj",
    )(x, g.reshape(1, d), w, w_dt)


def _residue_permutation(d):
    per_res = ROW_SUB // d
    out_row = lax.broadcasted_iota(jnp.int32, (ROW_SUB, ROW_SUB), 0)
    in_row = lax.broadcasted_iota(jnp.int32, (ROW_SUB, ROW_SUB), 1)
    src = jnp.bitwise_and(out_row, per_res - 1) * d + jnp.right_shift(out_row, per_res.bit_length() - 1)
    return jnp.where(in_row == src, 1.0, 0.0).astype(BF16)


def _norm_matmul_dil_kernel(x_ref, g_ref, w_ref, cos_ref, sin_ref, o_ref, xn_ref, acc_ref, *, dilation, rope,
                            col_tiles):
    d = dilation
    tn = w_ref.shape[1]
    per_res = ROW_SUB // d
    permute_rows = d > 1 and per_res <= 2 * SUBLANES

    def residue_major(ref, k):
        return jnp.concatenate([ref[pl.ds(k * ROW_SUB + r, per_res, stride=d), :] for r in range(d)], axis=0)

    def body(first):
        perm = _residue_permutation(d) if (first and permute_rows) else None
        for k, rows in enumerate(_row_subtiles(x_ref.shape[0])):
            if first and permute_rows:
                xn = _rms(x_ref[rows, :], g_ref[...]).astype(BF16)
                xn = jnp.dot(perm, xn, preferred_element_type=F32).astype(BF16)
                xn_ref[rows, :] = xn
            else:
                xn = _normed_rows(x_ref, g_ref, xn_ref, rows, first)
            acc = jnp.dot(xn, w_ref[...], preferred_element_type=F32)
            if rope:
                cos = residue_major(cos_ref, k) if permute_rows else cos_ref[rows, :]
                sin = residue_major(sin_ref, k) if permute_rows else sin_ref[rows, :]
            for hh in range(tn // DIL_HEAD_DIM):
                sl = slice(hh * DIL_HEAD_DIM, (hh + 1) * DIL_HEAD_DIM)
                t = acc[:, sl]
                if rope:
                    t = t * cos + pltpu.roll(t, DIL_HEAD_DIM // 2, 1) * sin
                if d == 1:
                    o_ref[0, rows, sl] = t.astype(BF16)
                elif permute_rows:
                    t16 = t.astype(BF16)
                    for r in range(d):
                        o_ref[r, k * per_res:(k + 1) * per_res, sl] = t16[r * per_res:(r + 1) * per_res, :]
                else:
                    acc_ref[k, hh] = t
                    for r in range(d):
                        o_ref[r, k * per_res:(k + 1) * per_res, sl] = (
                            acc_ref[k, hh, pl.ds(r, per_res, stride=d), :].astype(BF16))

    if col_tiles == 1:
        body(True)
    else:
        _first_step_variants(pl.program_id(1), body)


def _norm_matmul_dil(x, g, w, layer, rope_tables, *, col_block, dilation, rope, batch, seq, tm, tn, name):
    t, d_model = x.shape
    width = DIL_HEADS * DIL_HEAD_DIM
    tiles = width // tn
    per_b = seq // tm
    d = dilation
    strided_relayout = d > 1 and ROW_SUB // d > 2 * SUBLANES
    acc_shape = ((tm // ROW_SUB, tn // DIL_HEAD_DIM, ROW_SUB, DIL_HEAD_DIM) if strided_relayout
                 else (1, 1, SUBLANES, DIL_HEAD_DIM))
    return pl.pallas_call(
        functools.partial(_norm_matmul_dil_kernel, dilation=d, rope=rope, col_tiles=tiles),
        grid=(t // tm, tiles),
        in_specs=[pl.BlockSpec((tm, d_model), lambda i, j: (i, 0)),
                  pl.BlockSpec((1, d_model), lambda i, j: (0, 0)),
                  pl.BlockSpec((None, d_model, tn), lambda i, j: (layer, 0, col_block * tiles + j)),
                  pl.BlockSpec((tm, DIL_HEAD_DIM), lambda i, j: (i, 0)),
                  pl.BlockSpec((tm, DIL_HEAD_DIM), lambda i, j: (i, 0))],
        out_specs=pl.BlockSpec((None, d, tm // d, tn), lambda i, j: (i // per_b, 0, i % per_b, j)),
        out_shape=jax.ShapeDtypeStruct((batch, d, seq // d, width), BF16),
        scratch_shapes=[pltpu.VMEM((tm, d_model), BF16), pltpu.VMEM(acc_shape, F32)],
        compiler_params=_cparams("parallel", "arbitrary"),
        name=name,
    )(x, g.reshape(1, d_model), w, *rope_tables)


def _matmul_residual_kernel(a_ref, w_ref, h_ref, o_ref):
    o_ref[...] = h_ref[...] + jnp.dot(a_ref[...], w_ref[...], preferred_element_type=F32)


def _matmul_residual(a, w, layer, h, *, tm, tn, name="matmul_residual"):
    t, k = a.shape
    n = w.shape[2]
    return pl.pallas_call(
        _matmul_residual_kernel,
        grid=(n // tn, t // tm),
        in_specs=[pl.BlockSpec((tm, k), lambda j, i: (i, 0)),
                  pl.BlockSpec((None, k, tn), lambda j, i: (layer, 0, j)),
                  pl.BlockSpec((tm, tn), lambda j, i: (i, j))],
        out_specs=pl.BlockSpec((tm, tn), lambda j, i: (i, j)),
        out_shape=jax.ShapeDtypeStruct((t, n), F32),
        compiler_params=_cparams("parallel", "parallel"),
        name=name,
    )(a, w, h)


def _ssd_kernel(z_ref, xs_ref, b_ref, c_ref, dt_ref, convw_ref, convb_ref, dtb_ref, aneg_ref,
                dexp_ref, nw_ref, o_ref, xbuf, xc, state, *, chunks_per_seq):
    L = SSD_CHUNK
    d_inner = xs_ref.shape[1]
    bc_dim = b_ref.shape[1]
    conv_dim = d_inner + 2 * bc_dim
    gw = d_inner // SSD_GROUPS
    halo = SUBLANES
    first = pl.program_id(0) % chunks_per_seq == 0

    @pl.when(first)
    def _():
        xbuf[:, 0:halo, :] = jnp.zeros((xbuf.shape[0], halo, LANES), F32)
        state[...] = jnp.zeros(state.shape, F32)

    @pl.when(jnp.logical_not(first))
    def _():
        xbuf[:, 0:halo, :] = xbuf[:, L:L + halo, :]

    n_cb = conv_dim // LANES
    for cb in range(n_cb):
        c0 = cb * LANES
        if c0 < d_inner:
            src = xs_ref[:, c0:c0 + LANES]
        elif c0 < d_inner + bc_dim:
            src = b_ref[:, c0 - d_inner:c0 - d_inner + LANES]
        else:
            src = c_ref[:, c0 - d_inner - bc_dim:c0 - d_inner - bc_dim + LANES]
        xbuf[cb, halo:halo + L, :] = src

    nres = SUBLANES
    rows = L // nres
    for cb in range(n_cb):
        sl = slice(cb * LANES, (cb + 1) * LANES)
        taps = [convw_ref[k:k + 1, sl] for k in range(SSD_CONV)]
        bias = convb_ref[:, sl]
        cur = [xbuf[cb, pl.ds(halo + r, rows, stride=nres), :] for r in range(nres)]
        prv = {r: xbuf[cb, pl.ds(r, rows, stride=nres), :]
               for r in range(nres - SSD_CONV + 1, nres)}
        for r in range(nres):
            acc = bias
            for k in range(SSD_CONV):
                q = r - (SSD_CONV - 1) + k
                acc = acc + taps[k] * (cur[q] if q >= 0 else prv[q + nres])
            xc[cb, pl.ds(r, rows, stride=nres), :] = _silu(acc)

    raw = dt_ref[...] + dtb_ref[...]
    dt = jnp.maximum(raw, 0.0) + jnp.log1p(jnp.exp(-jnp.abs(raw)))
    acum = dt * aneg_ref[...]
    row = lax.broadcasted_iota(jnp.int32, (L, LANES), 0)
    sh = 1
    while sh < L:
        acum = acum + jnp.where(row >= sh, pltpu.roll(acum, sh, 0), 0.0)
        sh *= 2
    acum2 = acum * LOG2E
    acum2_t = acum2.T
    dt_t = dt.T
    ea = jnp.exp(acum)
    wend = jnp.exp(acum[L - 1:L, :] - acum) * dt

    ti = lax.broadcasted_iota(jnp.int32, (L, L), 0)
    si = lax.broadcasted_iota(jnp.int32, (L, L), 1)
    causal = ti >= si
    lo_half = si < SSD_HEAD_DIM

    def pair_expand(v, h0):
        return jnp.where(lo_half, v[:, h0:h0 + 1], v[:, h0 + 1:h0 + 2])

    heads_per_group = gw // SSD_HEAD_DIM
    for g in range(SSD_GROUPS):
        bg = xc[(d_inner + g * SSD_STATE) // LANES]
        cg = xc[(d_inner + bc_dim + g * SSD_STATE) // LANES]
        cb16 = cg.astype(BF16)
        cb = lax.dot_general(cb16, bg.astype(BF16), (((1,), (1,)), ((), ())),
                             preferred_element_type=F32)
        bt16 = bg.T.astype(BF16)
        st_old = state[g]
        yoff = jnp.dot(cb16, st_old.astype(BF16), preferred_element_type=F32)
        ys, xws, cds = [], [], []
        for pp in range(heads_per_group // 2):
            h0 = g * heads_per_group + 2 * pp
            c0 = g * gw + pp * LANES
            xpair = xc[c0 // LANES]
            xp16 = xpair.astype(BF16)
            yd = []
            for hh in (h0, h0 + 1):
                seg = acum2[:, hh:hh + 1] - acum2_t[hh:hh + 1, :]
                dec = jnp.exp2(jnp.where(causal, seg, -jnp.inf))
                m = (cb * dec * dt_t[hh:hh + 1, :]).astype(BF16)
                yd.append(jnp.dot(m, xp16, preferred_element_type=F32))
            ea_pair = pair_expand(ea, h0)
            y = (jnp.where(lo_half, yd[0], yd[1])
                 + yoff[:, pp * LANES:(pp + 1) * LANES] * ea_pair
                 + xpair * dexp_ref[:, c0:c0 + LANES])
            y = y * _silu(z_ref[:, c0:c0 + LANES])
            ys.append(y)
            xws.append((xpair * pair_expand(wend, h0)).astype(BF16))
            cds.append(ea_pair[L - 1:L, :])
        xw = jnp.concatenate(xws, axis=1)
        cd = jnp.concatenate(cds, axis=1)
        state[g] = st_old * cd + jnp.dot(bt16, xw, preferred_element_type=F32)
        yg = jnp.concatenate(ys, axis=1)
        ms = jnp.mean(yg * yg, axis=-1, keepdims=True)
        yn = yg * lax.rsqrt(ms + NORM_EPS) * nw_ref[:, g * gw:(g + 1) * gw]
        o_ref[:, g * gw:(g + 1) * gw] = yn.astype(o_ref.dtype)


def _ssd_core(proj, dt_raw, conv_w, conv_b, dt_bias, a_log, d_skip, norm_w, *, seq, d_inner):
    t = proj.shape[0]
    L = SSD_CHUNK
    bc_dim = SSD_GROUPS * SSD_STATE
    conv_dim = d_inner + 2 * bc_dim
    heads = d_inner // SSD_HEAD_DIM
    pad = LANES - heads

    def padrow(v):
        return jnp.pad(v.astype(F32), (0, pad)).reshape(1, LANES)

    aneg = padrow(-jnp.exp(a_log.astype(F32)))
    dexp = jnp.repeat(d_skip.astype(F32), SSD_HEAD_DIM).reshape(1, d_inner)
    xs_blk = d_inner // d_inner
    b_blk = (2 * d_inner) // bc_dim
    c_blk = b_blk + 1
    const = lambda c: (0, 0)
    return pl.pallas_call(
        functools.partial(_ssd_kernel, chunks_per_seq=seq // L),
        grid=(t // L,),
        in_specs=[pl.BlockSpec((L, d_inner), lambda c: (c, 0)),
                  pl.BlockSpec((L, d_inner), lambda c: (c, xs_blk)),
                  pl.BlockSpec((L, bc_dim), lambda c: (c, b_blk)),
                  pl.BlockSpec((L, bc_dim), lambda c: (c, c_blk)),
                  pl.BlockSpec((L, LANES), lambda c: (c, 0)),
                  pl.BlockSpec((SSD_CONV, conv_dim), const),
                  pl.BlockSpec((1, conv_dim), const),
                  pl.BlockSpec((1, LANES), const),
                  pl.BlockSpec((1, LANES), const),
                  pl.BlockSpec((1, d_inner), const),
                  pl.BlockSpec((1, d_inner), const)],
        out_specs=pl.BlockSpec((L, d_inner), lambda c: (c, 0)),
        out_shape=jax.ShapeDtypeStruct((t, d_inner), BF16),
        scratch_shapes=[pltpu.VMEM((conv_dim // LANES, L + 2 * SUBLANES, LANES), F32),
                        pltpu.VMEM((conv_dim // LANES, L, LANES), F32),
                        pltpu.VMEM((SSD_GROUPS, SSD_STATE, d_inner // SSD_GROUPS), F32)],
        compiler_params=_cparams("arbitrary"),
        name="ssd_core",
    )(proj, proj, proj, proj, dt_raw, conv_w.astype(F32), conv_b.reshape(1, conv_dim).astype(F32),
      padrow(dt_bias), aneg, dexp, norm_w.reshape(1, d_inner).astype(F32))


def _dil_attn_kernel(q_ref, k_ref, v_ref, o_ref, st_ref, kprev, vprev, bias_ref):
    blk = DIL_BLOCK
    n_blk = q_ref.shape[0] // blk
    i = pl.program_id(1)

    @pl.when(i == 0)
    def _():
        kprev[...] = jnp.zeros(kprev.shape, kprev.dtype)
        vprev[...] = jnp.zeros(vprev.shape, vprev.dtype)

    qi = lax.broadcasted_iota(jnp.int32, (blk, 2 * blk), 0)
    ki = lax.broadcasted_iota(jnp.int32, (blk, 2 * blk), 1)
    dist = qi + blk - ki
    band = (dist >= 0) & (dist <= blk)
    first_key = jnp.where(i > 0, 0, blk)
    bias_ref[0] = jnp.where(band & (ki >= first_key), 0.0, -jnp.inf)
    bias_ref[1] = jnp.where(band, 0.0, -jnp.inf)
    st_ref[...] = jnp.zeros(st_ref.shape, F32)
    scale = DIL_HEAD_DIM ** -0.5
    for j in range(n_blk):
        rows = slice(j * blk, (j + 1) * blk)
        prev_rows = slice((j - 1) * blk, j * blk)
        for h in range(DIL_HEADS):
            sl = slice(h * DIL_HEAD_DIM, (h + 1) * DIL_HEAD_DIM)
            k_prev = kprev[:, sl] if j == 0 else k_ref[prev_rows, sl]
            v_prev = vprev[:, sl] if j == 0 else v_ref[prev_rows, sl]
            k = jnp.concatenate([k_prev, k_ref[rows, sl]], axis=0)
            v = jnp.concatenate([v_prev, v_ref[rows, sl]], axis=0)
            raw = lax.dot_general(q_ref[rows, sl], k, (((1,), (1,)), ((), ())), preferred_element_type=F32)
            raw = raw + bias_ref[min(j, 1)]
            mx = jnp.max(raw, axis=-1, keepdims=True)
            p = jnp.exp2((raw - mx) * (scale * LOG2E))
            o_ref[rows, sl] = jnp.dot(p.astype(BF16), v, preferred_element_type=F32)
            st_ref[rows, h:h + 1] = mx * scale
            st_ref[rows, DIL_HEADS + h:DIL_HEADS + h + 1] = jnp.sum(p, axis=-1, keepdims=True)
    last = slice((n_blk - 1) * blk, n_blk * blk)
    kprev[...] = k_ref[last, :]
    vprev[...] = v_ref[last, :]


def _dil_attn(q, k, v):
    b, d, n, w = q.shape
    seqs = b * d
    rows = min(n, ATTN_BLOCKS_PER_STEP * DIL_BLOCK)
    blk = pl.BlockSpec((None, rows, w), lambda s, i: (s, i, 0))
    o, st = pl.pallas_call(
        _dil_attn_kernel,
        grid=(seqs, n // rows),
        in_specs=[blk, blk, blk],
        out_specs=[blk, pl.BlockSpec((None, rows, LANES), lambda s, i: (s, i, 0))],
        out_shape=[jax.ShapeDtypeStruct((seqs, n, w), F32),
                   jax.ShapeDtypeStruct((seqs, n, LANES), F32)],
        scratch_shapes=[pltpu.VMEM((DIL_BLOCK, w), BF16), pltpu.VMEM((DIL_BLOCK, w), BF16),
                        pltpu.VMEM((2, DIL_BLOCK, 2 * DIL_BLOCK), F32)],
        compiler_params=_cparams("arbitrary", "arbitrary"),
        name=f"dil_attn_d{d}",
    )(q.reshape(seqs, n, w), k.reshape(seqs, n, w), v.reshape(seqs, n, w))
    return o.reshape(b, d, n, w), st.reshape(b, d, n, LANES)


def _dil_combine_kernel(*refs, dilations):
    n_g = len(dilations)
    o_refs, l_refs = refs[:n_g], refs[n_g:2 * n_g]
    w_ref, h_ref, out_ref, onat, lnat = refs[2 * n_g:]
    tm = h_ref.shape[0]

    def head_slice(h):
        return slice(h * DIL_HEAD_DIM, (h + 1) * DIL_HEAD_DIM)

    o_heads, ls = [], []
    slot = 0
    for gi, d in enumerate(dilations):
        if d == 1:
            o_heads.append(lambda h, rows, gi=gi: o_refs[gi][0, rows, head_slice(h)])
            ls.append(lambda rows, gi=gi: l_refs[gi][0, rows, :])
            continue
        for r in range(d):
            strided = pl.ds(r, tm // d, stride=d)
            lnat[slot, strided, :] = l_refs[gi][r]
            for h in range(DIL_HEADS):
                onat[slot * DIL_HEADS + h, strided, :] = o_refs[gi][r, :, head_slice(h)]
        o_heads.append(lambda h, rows, slot=slot: onat[slot * DIL_HEADS + h, rows, :])
        ls.append(lambda rows, slot=slot: lnat[slot, rows, :])
        slot += 1
    half = tm // 2
    for rows in (slice(0, half), slice(half, tm)):
        stats = [l(rows) for l in ls]
        dens = [pltpu.roll(st, LANES - DIL_HEADS, 1) for st in stats]
        mx = functools.reduce(jnp.maximum, stats)
        es = [jnp.exp(st - mx) for st in stats]
        tot = functools.reduce(lambda u, v: u + v, [dn * e for dn, e in zip(dens, es)])
        ws = [e / tot for e in es]
        cols = []
        for h in range(DIL_HEADS):
            c = ws[0][:, h:h + 1] * o_heads[0](h, rows)
            for gi in range(1, n_g):
                c = c + ws[gi][:, h:h + 1] * o_heads[gi](h, rows)
            cols.append(c.astype(BF16))
        a = jnp.concatenate(cols, axis=1)
        out_ref[rows, :] = h_ref[rows, :] + jnp.dot(a, w_ref[...], preferred_element_type=F32)


def _dil_combine(os_, lses, w_o, layer, h, *, seq, tm):
    t, d_model = h.shape
    w = w_o.shape[1]
    dilations = tuple(o.shape[1] for o in os_)
    per_b = seq // tm
    n_strided = sum(1 for d in dilations if d > 1)

    def res_spec(d, width):
        return pl.BlockSpec((None, d, tm // d, width), lambda i: (i // per_b, 0, i % per_b, 0))

    row = lambda i: (i, 0)
    return pl.pallas_call(
        functools.partial(_dil_combine_kernel, dilations=dilations),
        grid=(t // tm,),
        in_specs=[res_spec(d, w) for d in dilations] + [res_spec(d, LANES) for d in dilations]
                 + [pl.BlockSpec((None, w, d_model), lambda i: (layer, 0, 0)), pl.BlockSpec((tm, d_model), row)],
        out_specs=pl.BlockSpec((tm, d_model), row),
        out_shape=jax.ShapeDtypeStruct((t, d_model), F32),
        scratch_shapes=[pltpu.VMEM((n_strided * DIL_HEADS, tm, DIL_HEAD_DIM), F32),
                        pltpu.VMEM((n_strided, tm, LANES), F32)],
        compiler_params=_cparams("parallel"),
        name="dil_combine",
    )(*os_, *lses, w_o, h)


def _mem_kv_kernel(mem_ref, g_ref, w_ref, o_ref):
    xn = _rms(mem_ref[...], g_ref[...]).astype(BF16)
    o_ref[...] = jnp.dot(xn, w_ref[...], preferred_element_type=F32).astype(o_ref.dtype)


def _mem_kv(mem, g, w_kv):
    b, m, d = mem.shape
    depth, _, n = w_kv.shape
    return pl.pallas_call(
        _mem_kv_kernel,
        grid=(depth, b),
        in_specs=[pl.BlockSpec((None, m, d), lambda l, bb: (bb, 0, 0)),
                  pl.BlockSpec((1, d), lambda l, bb: (0, 0)),
                  pl.BlockSpec((None, d, n), lambda l, bb: (l, 0, 0))],
        out_specs=pl.BlockSpec((None, None, m, n), lambda l, bb: (l, bb, 0, 0)),
        out_shape=jax.ShapeDtypeStruct((depth, b, m, n), BF16),
        compiler_params=_cparams("parallel", "parallel"),
        name="mem_kv",
    )(mem, g.reshape(1, d), w_kv)


def _mem_attn_kernel(h_ref, g_ref, wq_ref, kv_ref, wo_ref, o_ref):
    width = MEM_HEADS * MEM_HEAD_DIM
    scale = MEM_HEAD_DIM ** -0.5
    x = h_ref[...]
    xn = _rms(x, g_ref[...]).astype(BF16)
    q = jnp.dot(xn, wq_ref[...], preferred_element_type=F32)
    outs = []
    for hh in range(MEM_HEADS):
        sl = slice(hh * MEM_HEAD_DIM, (hh + 1) * MEM_HEAD_DIM)
        k = kv_ref[:, sl]
        v = kv_ref[:, width + hh * MEM_HEAD_DIM:width + (hh + 1) * MEM_HEAD_DIM]
        sc = lax.dot_general(q[:, sl].astype(BF16), k, (((1,), (1,)), ((), ())),
                             preferred_element_type=F32) * scale
        mx = jnp.max(sc, axis=-1, keepdims=True)
        p = jnp.exp(sc - mx)
        p = p / jnp.sum(p, axis=-1, keepdims=True)
        outs.append(jnp.dot(p.astype(BF16), v, preferred_element_type=F32).astype(BF16))
    a = jnp.concatenate(outs, axis=1)
    o_ref[...] = x + jnp.dot(a, wo_ref[...], preferred_element_type=F32)


def _mem_attn(h, g, w_q, kv, w_o, layer, *, batch, seq, tm):
    t, d = h.shape
    width = MEM_HEADS * MEM_HEAD_DIM
    m = kv.shape[2]
    per_b = seq // tm
    return pl.pallas_call(
        _mem_attn_kernel,
        grid=(t // tm,),
        in_specs=[pl.BlockSpec((tm, d), lambda i: (i, 0)),
                  pl.BlockSpec((1, d), lambda i: (0, 0)),
                  pl.BlockSpec((None, d, width), lambda i: (layer, 0, 0)),
                  pl.BlockSpec((None, None, m, 2 * width), lambda i: (layer, i // per_b, 0, 0)),
                  pl.BlockSpec((None, width, d), lambda i: (layer, 0, 0))],
        out_specs=pl.BlockSpec((tm, d), lambda i: (i, 0)),
        out_shape=jax.ShapeDtypeStruct((t, d), F32),
        compiler_params=_cparams("parallel"),
        name="mem_attn",
    )(h, g.reshape(1, d), w_q, kv, w_o)


def _ffn_kernel(h_ref, g_ref, wg_ref, wu_ref, wo_ref, gf_ref, o_ref, xn_ref, *, final_norm):
    f = pl.program_id(1)

    def body(first):
        for rows in _row_subtiles(h_ref.shape[0]):
            xn = _normed_rows(h_ref, g_ref, xn_ref, rows, first)
            gate = jnp.dot(xn, wg_ref[...], preferred_element_type=F32)
            up = jnp.dot(xn, wu_ref[...], preferred_element_type=F32)
            a = (_silu(gate) * up).astype(BF16)
            base = h_ref[rows, :] if first else o_ref[rows, :]
            o_ref[rows, :] = base + jnp.dot(a, wo_ref[...], preferred_element_type=F32)

    _first_step_variants(f, body)

    if final_norm:
        @pl.when(f == pl.num_programs(1) - 1)
        def _():
            o_ref[...] = _rms(o_ref[...], gf_ref[...])


def _ffn(h, g, w_in, w_out, layer, g_final, *, tm, tf, final_norm):
    t, d = h.shape
    d_ff = w_out.shape[1]
    nf = d_ff // tf
    return pl.pallas_call(
        functools.partial(_ffn_kernel, final_norm=final_norm),
        grid=(t // tm, nf),
        in_specs=[pl.BlockSpec((tm, d), lambda i, f: (i, 0)),
                  pl.BlockSpec((1, d), lambda i, f: (0, 0)),
                  pl.BlockSpec((None, d, tf), lambda i, f: (layer, 0, f)),
                  pl.BlockSpec((None, d, tf), lambda i, f: (layer, 0, nf + f)),
                  pl.BlockSpec((None, tf, d), lambda i, f: (layer, f, 0)),
                  pl.BlockSpec((1, d), lambda i, f: (0, 0))],
        out_specs=pl.BlockSpec((tm, d), lambda i, f: (i, 0)),
        out_shape=jax.ShapeDtypeStruct((t, d), F32),
        scratch_shapes=[pltpu.VMEM((tm, d), BF16)],
        compiler_params=_cparams("parallel", "arbitrary"),
        name="ffn",
    )(h, g.reshape(1, d), w_in, w_in, w_out, g_final.reshape(1, d))


def kernel(x, mem, positions, norm_mix, norm_mem, norm_ffn, norm_final, ssd_w_in, ssd_conv_w, ssd_conv_b, ssd_dt_bias, ssd_a_log, ssd_d, ssd_norm, ssd_w_out, kv_norm, w_kv_shared, dil_w_q, dil_w_o, mem_src_norm, mem_w_q, mem_w_kv, mem_w_o, ffn_w_in, ffn_w_out):
    batch, seq, d_model = x.shape
    depth = norm_mix.shape[0]
    n_a = ssd_w_in.shape[0]
    d_inner = ssd_w_out.shape[1]
    heads = d_inner // SSD_HEAD_DIM
    zx_dim = ssd_w_in.shape[2] - heads
    dil_w = DIL_HEADS * DIL_HEAD_DIM
    n_dil = len(DIL_PATTERNS)
    t = batch * seq

    h = x.reshape(t, d_model)
    rope = _rope_tables(positions)
    mem_kv = _mem_kv(mem, mem_src_norm, mem_w_kv.astype(BF16))
    mem_w_q16 = mem_w_q.astype(BF16)
    mem_w_o16 = mem_w_o.astype(BF16)
    ffn_w_in16 = ffn_w_in.astype(BF16)
    ffn_w_out16 = ffn_w_out.astype(BF16)
    ssd_w_in16 = ssd_w_in.astype(BF16)
    ssd_w_out16 = ssd_w_out.astype(BF16)
    w_kv16 = w_kv_shared.astype(BF16)[None]
    dil_w_q16 = dil_w_q.astype(BF16)
    dil_w_o16 = dil_w_o.astype(BF16)
    ssd_w_dt16 = jnp.pad(ssd_w_in[:, :, zx_dim:], ((0, 0), (0, 0), (0, LANES - heads))).astype(BF16)

    kv_sh = None
    for i in range(depth):
        if i < n_a:
            proj, dt_raw = _ssd_in_proj(h, norm_mix[i], ssd_w_in16, ssd_w_dt16, i, n_cols=zx_dim,
                                        tm=1024, tn=1024)
            y = _ssd_core(proj, dt_raw, ssd_conv_w[i], ssd_conv_b[i], ssd_dt_bias[i], ssd_a_log[i],
                          ssd_d[i], ssd_norm[i], seq=seq, d_inner=d_inner)
            h = _matmul_residual(y, ssd_w_out16, i, h, tm=512, tn=1024, name="ssd_out_proj")
        else:
            dils = tuple(d for _, d in DIL_PATTERNS)
            assert all(w // d == DIL_BLOCK for w, d in DIL_PATTERNS)
            proj = functools.partial(_norm_matmul_dil, batch=batch, seq=seq, tm=1024, tn=dil_w)
            if kv_sh is None:
                kv_sh = [proj(h, kv_norm, w_kv16, 0, rope, col_block=c, dilation=dils[c % n_dil],
                              rope=c < n_dil, name=f"shared_kv_proj_{c}") for c in range(2 * n_dil)]
            j = i - n_a
            qs = [proj(h, norm_mix[i], dil_w_q16, j, rope, col_block=c, dilation=dils[c], rope=True,
                       name=f"dil_q_proj_{c}") for c in range(n_dil)]
            outs, lses = [], []
            for g in range(n_dil):
                o, lse = _dil_attn(qs[g], kv_sh[g], kv_sh[n_dil + g])
                outs.append(o)
                lses.append(lse)
            h = _dil_combine(outs, lses, dil_w_o16, j, h, seq=seq, tm=256)
        h = _mem_attn(h, norm_mem[i], mem_w_q16, mem_kv, mem_w_o16, i, batch=batch, seq=seq, tm=1024)
        h = _ffn(h, norm_ffn[i], ffn_w_in16, ffn_w_out16, i, norm_final, tm=1024, tf=512,
                 final_norm=(i == depth - 1))
    return h.reshape(batch, seq, d_model)
```

```python
import functools
import math

import jax
import jax.numpy as jnp
import numpy as np
from jax import lax
from jax.experimental import pallas as pl
from jax.experimental.pallas import tpu as pltpu

F32 = jnp.float32
BF16 = jnp.bfloat16

NORM_EPS = 1e-6
SSD_HEAD_DIM = 64
SSD_GROUPS = 8
SSD_STATE = 128
SSD_CONV = 4
SSD_CHUNK = 128
DIL_PATTERNS = ((128, 1), (512, 4), (2048, 16))
DIL_HEADS = 16
DIL_HEAD_DIM = 128
DIL_BLOCK = 128
ROPE_THETA = 10000.0
LOG2E = math.log2(math.e)
MEM_HEADS = 4
MEM_HEAD_DIM = 128

LANES = 128
SUBLANES = 8
VMEM_LIMIT = 56 * 1024 * 1024
ROW_SUB = 256
FFN_ROW_SUB = 512
ATTN_BLOCKS_PER_STEP = 8


def _cparams(*sem):
    return pltpu.CompilerParams(dimension_semantics=sem, vmem_limit_bytes=VMEM_LIMIT)


def _rms(x, g):
    ms = jnp.mean(x * x, axis=-1, keepdims=True)
    return x * lax.rsqrt(ms + NORM_EPS) * g


def _silu(x):
    return x / (1.0 + jnp.exp(-x))


def _rope_table_kernel(pos_ref, freq_ref, cos_ref, sin_ref):
    ang = pos_ref[...].astype(F32) * freq_ref[...]
    lane = lax.broadcasted_iota(jnp.int32, ang.shape, 1)
    s = jnp.sin(ang)
    cos_ref[...] = jnp.cos(ang)
    sin_ref[...] = jnp.where(lane < DIL_HEAD_DIM // 2, -s, s)


def _rope_tables(positions):
    t = positions.size
    half = DIL_HEAD_DIM // 2
    inv_freq = ROPE_THETA ** (-jnp.arange(half, dtype=F32) / half)
    freq = jnp.concatenate([inv_freq, inv_freq]).reshape(1, DIL_HEAD_DIM)
    tm = 1024
    return pl.pallas_call(
        _rope_table_kernel,
        grid=(t // tm,),
        in_specs=[pl.BlockSpec((tm, 1), lambda i: (i, 0)),
                  pl.BlockSpec((1, DIL_HEAD_DIM), lambda i: (0, 0))],
        out_specs=[pl.BlockSpec((tm, DIL_HEAD_DIM), lambda i: (i, 0))] * 2,
        out_shape=[jax.ShapeDtypeStruct((t, DIL_HEAD_DIM), F32)] * 2,
        compiler_params=_cparams("parallel"),
        name="rope_tables",
    )(positions.reshape(t, 1), freq)


def _normed_rows(x_ref, g_ref, xn_ref, rows, first):
    if first:
        xn = _rms(x_ref[rows, :], g_ref[...]).astype(BF16)
        xn_ref[rows, :] = xn
        return xn
    return xn_ref[rows, :]


def _row_subtiles(tm, sub=ROW_SUB):
    return [slice(k * sub, (k + 1) * sub) for k in range(tm // sub)]


def _first_step_variants(step_id, body):
    pl.when(step_id == 0)(functools.partial(body, True))
    pl.when(step_id != 0)(functools.partial(body, False))


def _ssd_in_proj_kernel(x_ref, g_ref, w_ref, wdt_ref, o_ref, dt_ref, xn_ref):
    def body(first):
        for rows in _row_subtiles(x_ref.shape[0]):
            xn = _normed_rows(x_ref, g_ref, xn_ref, rows, first)
            o_ref[rows, :] = jnp.dot(xn, w_ref[...], preferred_element_type=F32)
            if first:
                dt_ref[rows, :] = jnp.dot(xn, wdt_ref[...], preferred_element_type=F32)

    _first_step_variants(pl.program_id(1), body)


def _ssd_in_proj(x, g, w, w_dt, layer, *, n_cols, tm, tn):
    t, d = x.shape
    return pl.pallas_call(
        _ssd_in_proj_kernel,
        grid=(t // tm, n_cols // tn),
        in_specs=[pl.BlockSpec((tm, d), lambda i, j: (i, 0)),
                  pl.BlockSpec((1, d), lambda i, j: (0, 0)),
                  pl.BlockSpec((None, d, tn), lambda i, j: (layer, 0, j)),
                  pl.BlockSpec((None, d, LANES), lambda i, j: (layer, 0, 0))],
        out_specs=[pl.BlockSpec((tm, tn), lambda i, j: (i, j)),
                   pl.BlockSpec((tm, LANES), lambda i, j: (i, 0))],
        out_shape=[jax.ShapeDtypeStruct((t, n_cols), F32), jax.ShapeDtypeStruct((t, LANES), F32)],
        scratch_shapes=[pltpu.VMEM((tm, d), BF16)],
        compiler_params=_cparams("parallel", "arbitrary"),
        name="ssd_in_proj",
    )(x, g.reshape(1, d), w, w_dt)


def _residue_permutation(d):
    per_res = ROW_SUB // d
    out_row = lax.broadcasted_iota(jnp.int32, (ROW_SUB, ROW_SUB), 0)
    in_row = lax.broadcasted_iota(jnp.int32, (ROW_SUB, ROW_SUB), 1)
    src = jnp.bitwise_and(out_row, per_res - 1) * d + jnp.right_shift(out_row, per_res.bit_length() - 1)
    return jnp.where(in_row == src, 1.0, 0.0).astype(BF16)


def _norm_matmul_dil_kernel(x_ref, g_ref, w_ref, cos_ref, sin_ref, o_ref, xn_ref, acc_ref, *, dilation, rope,
                            col_tiles):
    d = dilation
    tn = w_ref.shape[1]
    per_res = ROW_SUB // d
    permute_rows = d > 1 and per_res <= 2 * SUBLANES

    def residue_major(ref, k):
        return jnp.concatenate([ref[pl.ds(k * ROW_SUB + r, per_res, stride=d), :] for r in range(d)], axis=0)

    def body(first):
        perm = _residue_permutation(d) if (first and permute_rows) else None
        for k, rows in enumerate(_row_subtiles(x_ref.shape[0])):
            if first and permute_rows:
                xn = _rms(x_ref[rows, :], g_ref[...]).astype(BF16)
                xn = jnp.dot(perm, xn, preferred_element_type=F32).astype(BF16)
                xn_ref[rows, :] = xn
            else:
                xn = _normed_rows(x_ref, g_ref, xn_ref, rows, first)
            acc = jnp.dot(xn, w_ref[...], preferred_element_type=F32)
            if rope:
                cos = residue_major(cos_ref, k) if permute_rows else cos_ref[rows, :]
                sin = residue_major(sin_ref, k) if permute_rows else sin_ref[rows, :]
            for hh in range(tn // DIL_HEAD_DIM):
                sl = slice(hh * DIL_HEAD_DIM, (hh + 1) * DIL_HEAD_DIM)
                t = acc[:, sl]
                if rope:
                    t = t * cos + pltpu.roll(t, DIL_HEAD_DIM // 2, 1) * sin
                if d == 1:
                    o_ref[0, rows, sl] = t.astype(BF16)
                elif permute_rows:
                    t16 = t.astype(BF16)
                    for r in range(d):
                        o_ref[r, k * per_res:(k + 1) * per_res, sl] = t16[r * per_res:(r + 1) * per_res, :]
                else:
                    acc_ref[k, hh] = t
                    for r in range(d):
                        o_ref[r, k * per_res:(k + 1) * per_res, sl] = (
                            acc_ref[k, hh, pl.ds(r, per_res, stride=d), :].astype(BF16))

    if col_tiles == 1:
        body(True)
    else:
        _first_step_variants(pl.program_id(1), body)


def _norm_matmul_dil(x, g, w, layer, rope_tables, *, col_block, dilation, rope, batch, seq, tm, tn, name):
    t, d_model = x.shape
    width = DIL_HEADS * DIL_HEAD_DIM
    tiles = width // tn
    per_b = seq // tm
    d = dilation
    strided_relayout = d > 1 and ROW_SUB // d > 2 * SUBLANES
    acc_shape = ((tm // ROW_SUB, tn // DIL_HEAD_DIM, ROW_SUB, DIL_HEAD_DIM) if strided_relayout
                 else (1, 1, SUBLANES, DIL_HEAD_DIM))
    return pl.pallas_call(
        functools.partial(_norm_matmul_dil_kernel, dilation=d, rope=rope, col_tiles=tiles),
        grid=(t // tm, tiles),
        in_specs=[pl.BlockSpec((tm, d_model), lambda i, j: (i, 0)),
                  pl.BlockSpec((1, d_model), lambda i, j: (0, 0)),
                  pl.BlockSpec((None, d_model, tn), lambda i, j: (layer, 0, col_block * tiles + j)),
                  pl.BlockSpec((tm, DIL_HEAD_DIM), lambda i, j: (i, 0)),
                  pl.BlockSpec((tm, DIL_HEAD_DIM), lambda i, j: (i, 0))],
        out_specs=pl.BlockSpec((None, d, tm // d, tn), lambda i, j: (i // per_b, 0, i % per_b, j)),
        out_shape=jax.ShapeDtypeStruct((batch, d, seq // d, width), BF16),
        scratch_shapes=[pltpu.VMEM((tm, d_model), BF16), pltpu.VMEM(acc_shape, F32)],
        compiler_params=_cparams("parallel", "arbitrary"),
        name=name,
    )(x, g.reshape(1, d_model), w, *rope_tables)


def _matmul_residual_kernel(a_ref, w_ref, h_ref, o_ref):
    o_ref[...] = h_ref[...] + jnp.dot(a_ref[...], w_ref[...], preferred_element_type=F32)


def _matmul_residual(a, w, layer, h, *, tm, tn, name="matmul_residual"):
    t, k = a.shape
    n = w.shape[2]
    return pl.pallas_call(
        _matmul_residual_kernel,
        grid=(n // tn, t // tm),
        in_specs=[pl.BlockSpec((tm, k), lambda j, i: (i, 0)),
                  pl.BlockSpec((None, k, tn), lambda j, i: (layer, 0, j)),
                  pl.BlockSpec((tm, tn), lambda j, i: (i, j))],
        out_specs=pl.BlockSpec((tm, tn), lambda j, i: (i, j)),
        out_shape=jax.ShapeDtypeStruct((t, n), F32),
        compiler_params=_cparams("parallel", "parallel"),
        name=name,
    )(a, w, h)


def _ssd_kernel(z_ref, xs_ref, b_ref, c_ref, dt_ref, convw_ref, convb_ref, dtb_ref, aneg_ref,
                dexp_ref, nw_ref, o_ref, xbuf, xc, state, *, chunks_per_seq):
    L = SSD_CHUNK
    d_inner = xs_ref.shape[1]
    bc_dim = b_ref.shape[1]
    conv_dim = d_inner + 2 * bc_dim
    gw = d_inner // SSD_GROUPS
    halo = SUBLANES
    first = pl.program_id(0) % chunks_per_seq == 0

    @pl.when(first)
    def _():
        xbuf[:, 0:halo, :] = jnp.zeros((xbuf.shape[0], halo, LANES), F32)
        state[...] = jnp.zeros(state.shape, F32)

    @pl.when(jnp.logical_not(first))
    def _():
        xbuf[:, 0:halo, :] = xbuf[:, L:L + halo, :]

    n_cb = conv_dim // LANES
    for cb in range(n_cb):
        c0 = cb * LANES
        if c0 < d_inner:
            src = xs_ref[:, c0:c0 + LANES]
        elif c0 < d_inner + bc_dim:
            src = b_ref[:, c0 - d_inner:c0 - d_inner + LANES]
        else:
            src = c_ref[:, c0 - d_inner - bc_dim:c0 - d_inner - bc_dim + LANES]
        xbuf[cb, halo:halo + L, :] = src

    nres = SUBLANES
    rows = L // nres
    for cb in range(n_cb):
        sl = slice(cb * LANES, (cb + 1) * LANES)
        taps = [convw_ref[k:k + 1, sl] for k in range(SSD_CONV)]
        bias = convb_ref[:, sl]
        cur = [xbuf[cb, pl.ds(halo + r, rows, stride=nres), :] for r in range(nres)]
        prv = {r: xbuf[cb, pl.ds(r, rows, stride=nres), :]
               for r in range(nres - SSD_CONV + 1, nres)}
        for r in range(nres):
            acc = bias
            for k in range(SSD_CONV):
                q = r - (SSD_CONV - 1) + k
                acc = acc + taps[k] * (cur[q] if q >= 0 else prv[q + nres])
            xc[cb, pl.ds(r, rows, stride=nres), :] = _silu(acc)

    raw = dt_ref[...] + dtb_ref[...]
    dt = jnp.maximum(raw, 0.0) + jnp.log1p(jnp.exp(-jnp.abs(raw)))
    acum = dt * aneg_ref[...]
    row = lax.broadcasted_iota(jnp.int32, (L, LANES), 0)
    sh = 1
    while sh < L:
        acum = acum + jnp.where(row >= sh, pltpu.roll(acum, sh, 0), 0.0)
        sh *= 2
    acum2 = acum * LOG2E
    acum2_t = acum2.T
    dt_t = dt.T
    ea = jnp.exp(acum)
    wend = jnp.exp(acum[L - 1:L, :] - acum) * dt

    ti = lax.broadcasted_iota(jnp.int32, (L, L), 0)
    si = lax.broadcasted_iota(jnp.int32, (L, L), 1)
    causal = ti >= si
    lo_half = si < SSD_HEAD_DIM

    def pair_expand(v, h0):
        return jnp.where(lo_half, v[:, h0:h0 + 1], v[:, h0 + 1:h0 + 2])

    heads_per_group = gw // SSD_HEAD_DIM
    for g in range(SSD_GROUPS):
        bg = xc[(d_inner + g * SSD_STATE) // LANES]
        cg = xc[(d_inner + bc_dim + g * SSD_STATE) // LANES]
        cb16 = cg.astype(BF16)
        cb = lax.dot_general(cb16, bg.astype(BF16), (((1,), (1,)), ((), ())),
                             preferred_element_type=F32)
        bt16 = bg.T.astype(BF16)
        st_old = state[g]
        yoff = jnp.dot(cb16, st_old.astype(BF16), preferred_element_type=F32)
        ys, xws, cds = [], [], []
        for pp in range(heads_per_group // 2):
            h0 = g * heads_per_group + 2 * pp
            c0 = g * gw + pp * LANES
            xpair = xc[c0 // LANES]
            xp16 = xpair.astype(BF16)
            yd = []
            for hh in (h0, h0 + 1):
                seg = acum2[:, hh:hh + 1] - acum2_t[hh:hh + 1, :]
                dec = jnp.exp2(jnp.where(causal, seg, -jnp.inf))
                m = (cb * dec * dt_t[hh:hh + 1, :]).astype(BF16)
                yd.append(jnp.dot(m, xp16, preferred_element_type=F32))
            ea_pair = pair_expand(ea, h0)
            y = (jnp.where(lo_half, yd[0], yd[1])
                 + yoff[:, pp * LANES:(pp + 1) * LANES] * ea_pair
                 + xpair * dexp_ref[:, c0:c0 + LANES])
            y = y * _silu(z_ref[:, c0:c0 + LANES])
            ys.append(y)
            xws.append((xpair * pair_expand(wend, h0)).astype(BF16))
            cds.append(ea_pair[L - 1:L, :])
        xw = jnp.concatenate(xws, axis=1)
        cd = jnp.concatenate(cds, axis=1)
        state[g] = st_old * cd + jnp.dot(bt16, xw, preferred_element_type=F32)
        yg = jnp.concatenate(ys, axis=1)
        ms = jnp.mean(yg * yg, axis=-1, keepdims=True)
        yn = yg * lax.rsqrt(ms + NORM_EPS) * nw_ref[:, g * gw:(g + 1) * gw]
        o_ref[:, g * gw:(g + 1) * gw] = yn.astype(o_ref.dtype)


def _ssd_core(proj, dt_raw, conv_w, conv_b, dt_bias, a_log, d_skip, norm_w, *, seq, d_inner):
    t = proj.shape[0]
    L = SSD_CHUNK
    bc_dim = SSD_GROUPS * SSD_STATE
    conv_dim = d_inner + 2 * bc_dim
    heads = d_inner // SSD_HEAD_DIM
    pad = LANES - heads

    def padrow(v):
        return jnp.pad(v.astype(F32), (0, pad)).reshape(1, LANES)

    aneg = padrow(-jnp.exp(a_log.astype(F32)))
    dexp = jnp.repeat(d_skip.astype(F32), SSD_HEAD_DIM).reshape(1, d_inner)
    xs_blk = d_inner // d_inner
    b_blk = (2 * d_inner) // bc_dim
    c_blk = b_blk + 1
    const = lambda c: (0, 0)
    return pl.pallas_call(
        functools.partial(_ssd_kernel, chunks_per_seq=seq // L),
        grid=(t // L,),
        in_specs=[pl.BlockSpec((L, d_inner), lambda c: (c, 0)),
                  pl.BlockSpec((L, d_inner), lambda c: (c, xs_blk)),
                  pl.BlockSpec((L, bc_dim), lambda c: (c, b_blk)),
                  pl.BlockSpec((L, bc_dim), lambda c: (c, c_blk)),
                  pl.BlockSpec((L, LANES), lambda c: (c, 0)),
                  pl.BlockSpec((SSD_CONV, conv_dim), const),
                  pl.BlockSpec((1, conv_dim), const),
                  pl.BlockSpec((1, LANES), const),
                  pl.BlockSpec((1, LANES), const),
                  pl.BlockSpec((1, d_inner), const),
                  pl.BlockSpec((1, d_inner), const)],
        out_specs=pl.BlockSpec((L, d_inner), lambda c: (c, 0)),
        out_shape=jax.ShapeDtypeStruct((t, d_inner), BF16),
        scratch_shapes=[pltpu.VMEM((conv_dim // LANES, L + 2 * SUBLANES, LANES), F32),
                        pltpu.VMEM((conv_dim // LANES, L, LANES), F32),
                        pltpu.VMEM((SSD_GROUPS, SSD_STATE, d_inner // SSD_GROUPS), F32)],
        compiler_params=_cparams("arbitrary"),
        name="ssd_core",
    )(proj, proj, proj, proj, dt_raw, conv_w.astype(F32), conv_b.reshape(1, conv_dim).astype(F32),
      padrow(dt_bias), aneg, dexp, norm_w.reshape(1, d_inner).astype(F32))


def _dil_attn_kernel(q_ref, k_ref, v_ref, o_ref, st_ref, kprev, vprev, bias_ref):
    blk = DIL_BLOCK
    n_blk = q_ref.shape[0] // blk
    i = pl.program_id(1)

    @pl.when(i == 0)
    def _():
        kprev[...] = jnp.zeros(kprev.shape, kprev.dtype)
        vprev[...] = jnp.zeros(vprev.shape, vprev.dtype)

    qi = lax.broadcasted_iota(jnp.int32, (blk, 2 * blk), 0)
    ki = lax.broadcasted_iota(jnp.int32, (blk, 2 * blk), 1)
    dist = qi + blk - ki
    band = (dist >= 0) & (dist <= blk)
    first_key = jnp.where(i > 0, 0, blk)
    bias_ref[0] = jnp.where(band & (ki >= first_key), 0.0, -jnp.inf)
    bias_ref[1] = jnp.where(band, 0.0, -jnp.inf)
    st_ref[...] = jnp.zeros(st_ref.shape, F32)
    scale = DIL_HEAD_DIM ** -0.5
    for j in range(n_blk):
        rows = slice(j * blk, (j + 1) * blk)
        prev_rows = slice((j - 1) * blk, j * blk)
        for h in range(DIL_HEADS):
            sl = slice(h * DIL_HEAD_DIM, (h + 1) * DIL_HEAD_DIM)
            k_prev = kprev[:, sl] if j == 0 else k_ref[prev_rows, sl]
            v_prev = vprev[:, sl] if j == 0 else v_ref[prev_rows, sl]
            k = jnp.concatenate([k_prev, k_ref[rows, sl]], axis=0)
            v = jnp.concatenate([v_prev, v_ref[rows, sl]], axis=0)
            raw = lax.dot_general(q_ref[rows, sl], k, (((1,), (1,)), ((), ())), preferred_element_type=F32)
            raw = raw + bias_ref[min(j, 1)]
            mx = jnp.max(raw, axis=-1, keepdims=True)
            p = jnp.exp2((raw - mx) * (scale * LOG2E))
            o_ref[rows, sl] = jnp.dot(p.astype(BF16), v, preferred_element_type=F32)
            st_ref[rows, h:h + 1] = mx * scale
            st_ref[rows, DIL_HEADS + h:DIL_HEADS + h + 1] = jnp.sum(p, axis=-1, keepdims=True)
    last = slice((n_blk - 1) * blk, n_blk * blk)
    kprev[...] = k_ref[last, :]
    vprev[...] = v_ref[last, :]


def _dil_attn(q, k, v):
    b, d, n, w = q.shape
    seqs = b * d
    rows = min(n, ATTN_BLOCKS_PER_STEP * DIL_BLOCK)
    blk = pl.BlockSpec((None, rows, w), lambda s, i: (s, i, 0))
    o, st = pl.pallas_call(
        _dil_attn_kernel,
        grid=(seqs, n // rows),
        in_specs=[blk, blk, blk],
        out_specs=[blk, pl.BlockSpec((None, rows, LANES), lambda s, i: (s, i, 0))],
        out_shape=[jax.ShapeDtypeStruct((seqs, n, w), F32),
                   jax.ShapeDtypeStruct((seqs, n, LANES), F32)],
        scratch_shapes=[pltpu.VMEM((DIL_BLOCK, w), BF16), pltpu.VMEM((DIL_BLOCK, w), BF16),
                        pltpu.VMEM((2, DIL_BLOCK, 2 * DIL_BLOCK), F32)],
        compiler_params=_cparams("arbitrary", "arbitrary"),
        name=f"dil_attn_d{d}",
    )(q.reshape(seqs, n, w), k.reshape(seqs, n, w), v.reshape(seqs, n, w))
    return o.reshape(b, d, n, w), st.reshape(b, d, n, LANES)


def _dil_combine_kernel(*refs, dilations):
    n_g = len(dilations)
    o_refs, l_refs = refs[:n_g], refs[n_g:2 * n_g]
    w_ref, h_ref, out_ref, onat, lnat = refs[2 * n_g:]
    tm = h_ref.shape[0]

    def head_slice(h):
        return slice(h * DIL_HEAD_DIM, (h + 1) * DIL_HEAD_DIM)

    o_heads, ls = [], []
    slot = 0
    for gi, d in enumerate(dilations):
        if d == 1:
            o_heads.append(lambda h, rows, gi=gi: o_refs[gi][0, rows, head_slice(h)])
            ls.append(lambda rows, gi=gi: l_refs[gi][0, rows, :])
            continue
        for r in range(d):
            strided = pl.ds(r, tm // d, stride=d)
            lnat[slot, strided, :] = l_refs[gi][r]
            for h in range(DIL_HEADS):
                onat[slot * DIL_HEADS + h, strided, :] = o_refs[gi][r, :, head_slice(h)]
        o_heads.append(lambda h, rows, slot=slot: onat[slot * DIL_HEADS + h, rows, :])
        ls.append(lambda rows, slot=slot: lnat[slot, rows, :])
        slot += 1
    half = tm // 2
    for rows in (slice(0, half), slice(half, tm)):
        stats = [l(rows) for l in ls]
        dens = [pltpu.roll(st, LANES - DIL_HEADS, 1) for st in stats]
        mx = functools.reduce(jnp.maximum, stats)
        es = [jnp.exp(st - mx) for st in stats]
        tot = functools.reduce(lambda u, v: u + v, [dn * e for dn, e in zip(dens, es)])
        ws = [e / tot for e in es]
        cols = []
        for h in range(DIL_HEADS):
            c = ws[0][:, h:h + 1] * o_heads[0](h, rows)
            for gi in range(1, n_g):
                c = c + ws[gi][:, h:h + 1] * o_heads[gi](h, rows)
            cols.append(c.astype(BF16))
        a = jnp.concatenate(cols, axis=1)
        out_ref[rows, :] = h_ref[rows, :] + jnp.dot(a, w_ref[...], preferred_element_type=F32)


def _dil_combine(os_, lses, w_o, layer, h, *, seq, tm):
    t, d_model = h.shape
    w = w_o.shape[1]
    dilations = tuple(o.shape[1] for o in os_)
    per_b = seq // tm
    n_strided = sum(1 for d in dilations if d > 1)

    def res_spec(d, width):
        return pl.BlockSpec((None, d, tm // d, width), lambda i: (i // per_b, 0, i % per_b, 0))

    row = lambda i: (i, 0)
    return pl.pallas_call(
        functools.partial(_dil_combine_kernel, dilations=dilations),
        grid=(t // tm,),
        in_specs=[res_spec(d, w) for d in dilations] + [res_spec(d, LANES) for d in dilations]
                 + [pl.BlockSpec((None, w, d_model), lambda i: (layer, 0, 0)), pl.BlockSpec((tm, d_model), row)],
        out_specs=pl.BlockSpec((tm, d_model), row),
        out_shape=jax.ShapeDtypeStruct((t, d_model), F32),
        scratch_shapes=[pltpu.VMEM((n_strided * DIL_HEADS, tm, DIL_HEAD_DIM), F32),
                        pltpu.VMEM((n_strided, tm, LANES), F32)],
        compiler_params=_cparams("parallel"),
        name="dil_combine",
    )(*os_, *lses, w_o, h)


def _mem_kv_kernel(mem_ref, g_ref, w_ref, o_ref):
    xn = _rms(mem_ref[...], g_ref[...]).astype(BF16)
    o_ref[...] = jnp.dot(xn, w_ref[...], preferred_element_type=F32).astype(o_ref.dtype)


def _mem_kv(mem, g, w_kv):
    b, m, d = mem.shape
    depth, _, n = w_kv.shape
    return pl.pallas_call(
        _mem_kv_kernel,
        grid=(depth, b),
        in_specs=[pl.BlockSpec((None, m, d), lambda l, bb: (bb, 0, 0)),
                  pl.BlockSpec((1, d), lambda l, bb: (0, 0)),
                  pl.BlockSpec((None, d, n), lambda l, bb: (l, 0, 0))],
        out_specs=pl.BlockSpec((None, None, m, n), lambda l, bb: (l, bb, 0, 0)),
        out_shape=jax.ShapeDtypeStruct((depth, b, m, n), BF16),
        compiler_params=_cparams("parallel", "parallel"),
        name="mem_kv",
    )(mem, g.reshape(1, d), w_kv)


def _mem_attn_kernel(h_ref, g_ref, wq_ref, kv_ref, wo_ref, o_ref):
    width = MEM_HEADS * MEM_HEAD_DIM
    scale = MEM_HEAD_DIM ** -0.5
    x = h_ref[...]
    xn = _rms(x, g_ref[...]).astype(BF16)
    q = jnp.dot(xn, wq_ref[...], preferred_element_type=F32)
    outs = []
    for hh in range(MEM_HEADS):
        sl = slice(hh * MEM_HEAD_DIM, (hh + 1) * MEM_HEAD_DIM)
        k = kv_ref[:, sl]
        v = kv_ref[:, width + hh * MEM_HEAD_DIM:width + (hh + 1) * MEM_HEAD_DIM]
        sc = lax.dot_general(q[:, sl].astype(BF16), k, (((1,), (1,)), ((), ())),
                             preferred_element_type=F32) * scale
        mx = jnp.max(sc, axis=-1, keepdims=True)
        p = jnp.exp(sc - mx)
        p = p / jnp.sum(p, axis=-1, keepdims=True)
        outs.append(jnp.dot(p.astype(BF16), v, preferred_element_type=F32).astype(BF16))
    a = jnp.concatenate(outs, axis=1)
    o_ref[...] = x + jnp.dot(a, wo_ref[...], preferred_element_type=F32)


def _mem_attn(h, g, w_q, kv, w_o, layer, *, batch, seq, tm):
    t, d = h.shape
    width = MEM_HEADS * MEM_HEAD_DIM
    m = kv.shape[2]
    per_b = seq // tm
    return pl.pallas_call(
        _mem_attn_kernel,
        grid=(t // tm,),
        in_specs=[pl.BlockSpec((tm, d), lambda i: (i, 0)),
                  pl.BlockSpec((1, d), lambda i: (0, 0)),
                  pl.BlockSpec((None, d, width), lambda i: (layer, 0, 0)),
                  pl.BlockSpec((None, None, m, 2 * width), lambda i: (layer, i // per_b, 0, 0)),
                  pl.BlockSpec((None, width, d), lambda i: (layer, 0, 0))],
        out_specs=pl.BlockSpec((tm, d), lambda i: (i, 0)),
        out_shape=jax.ShapeDtypeStruct((t, d), F32),
        compiler_params=_cparams("parallel"),
        name="mem_attn",
    )(h, g.reshape(1, d), w_q, kv, w_o)


def _ffn_kernel(h_ref, g_ref, wg_ref, wu_ref, wo_ref, gf_ref, o_ref, xn_ref, *, final_norm):
    f = pl.program_id(1)

    def body(first):
        for rows in _row_subtiles(h_ref.shape[0], FFN_ROW_SUB):
            xn = _normed_rows(h_ref, g_ref, xn_ref, rows, first)
            gate = jnp.dot(xn, wg_ref[...], preferred_element_type=F32)
            up = jnp.dot(xn, wu_ref[...], preferred_element_type=F32)
            a = (_silu(gate) * up).astype(BF16)
            base = h_ref[rows, :] if first else o_ref[rows, :]
            o_ref[rows, :] = base + jnp.dot(a, wo_ref[...], preferred_element_type=F32)

    _first_step_variants(f, body)

    if final_norm:
        @pl.when(f == pl.num_programs(1) - 1)
        def _():
            o_ref[...] = _rms(o_ref[...], gf_ref[...])


def _ffn(h, g, w_in, w_out, layer, g_final, *, tm, tf, final_norm):
    t, d = h.shape
    d_ff = w_out.shape[1]
    nf = d_ff // tf
    return pl.pallas_call(
        functools.partial(_ffn_kernel, final_norm=final_norm),
        grid=(t // tm, nf),
        in_specs=[pl.BlockSpec((tm, d), lambda i, f: (i, 0)),
                  pl.BlockSpec((1, d), lambda i, f: (0, 0)),
                  pl.BlockSpec((None, d, tf), lambda i, f: (layer, 0, f)),
                  pl.BlockSpec((None, d, tf), lambda i, f: (layer, 0, nf + f)),
                  pl.BlockSpec((None, tf, d), lambda i, f: (layer, f, 0)),
                  pl.BlockSpec((1, d), lambda i, f: (0, 0))],
        out_specs=pl.BlockSpec((tm, d), lambda i, f: (i, 0)),
        out_shape=jax.ShapeDtypeStruct((t, d), F32),
        scratch_shapes=[pltpu.VMEM((tm, d), BF16)],
        compiler_params=_cparams("parallel", "arbitrary"),
        name="ffn",
    )(h, g.reshape(1, d), w_in, w_in, w_out, g_final.reshape(1, d))


def kernel(x, mem, positions, norm_mix, norm_mem, norm_ffn, norm_final, ssd_w_in, ssd_conv_w, ssd_conv_b, ssd_dt_bias, ssd_a_log, ssd_d, ssd_norm, ssd_w_out, kv_norm, w_kv_shared, dil_w_q, dil_w_o, mem_src_norm, mem_w_q, mem_w_kv, mem_w_o, ffn_w_in, ffn_w_out):
    batch, seq, d_model = x.shape
    depth = norm_mix.shape[0]
    n_a = ssd_w_in.shape[0]
    d_inner = ssd_w_out.shape[1]
    heads = d_inner // SSD_HEAD_DIM
    zx_dim = ssd_w_in.shape[2] - heads
    dil_w = DIL_HEADS * DIL_HEAD_DIM
    n_dil = len(DIL_PATTERNS)
    t = batch * seq

    h = x.reshape(t, d_model)
    rope = _rope_tables(positions)
    mem_kv = _mem_kv(mem, mem_src_norm, mem_w_kv.astype(BF16))
    mem_w_q16 = mem_w_q.astype(BF16)
    mem_w_o16 = mem_w_o.astype(BF16)
    ffn_w_in16 = ffn_w_in.astype(BF16)
    ffn_w_out16 = ffn_w_out.astype(BF16)
    ssd_w_in16 = ssd_w_in.astype(BF16)
    ssd_w_out16 = ssd_w_out.astype(BF16)
    w_kv16 = w_kv_shared.astype(BF16)[None]
    dil_w_q16 = dil_w_q.astype(BF16)
    dil_w_o16 = dil_w_o.astype(BF16)
    ssd_w_dt16 = jnp.pad(ssd_w_in[:, :, zx_dim:], ((0, 0), (0, 0), (0, LANES - heads))).astype(BF16)

    kv_sh = None
    for i in range(depth):
        if i < n_a:
            proj, dt_raw = _ssd_in_proj(h, norm_mix[i], ssd_w_in16, ssd_w_dt16, i, n_cols=zx_dim,
                                        tm=1024, tn=1024)
            y = _ssd_core(proj, dt_raw, ssd_conv_w[i], ssd_conv_b[i], ssd_dt_bias[i], ssd_a_log[i],
                          ssd_d[i], ssd_norm[i], seq=seq, d_inner=d_inner)
            h = _matmul_residual(y, ssd_w_out16, i, h, tm=512, tn=1024, name="ssd_out_proj")
        else:
            dils = tuple(d for _, d in DIL_PATTERNS)
            assert all(w // d == DIL_BLOCK for w, d in DIL_PATTERNS)
            proj = functools.partial(_norm_matmul_dil, batch=batch, seq=seq, tm=1024, tn=dil_w)
            if kv_sh is None:
                kv_sh = [proj(h, kv_norm, w_kv16, 0, rope, col_block=c, dilation=dils[c % n_dil],
                              rope=c < n_dil, name=f"shared_kv_proj_{c}") for c in range(2 * n_dil)]
            j = i - n_a
            qs = [proj(h, norm_mix[i], dil_w_q16, j, rope, col_block=c, dilation=dils[c], rope=True,
                       name=f"dil_q_proj_{c}") for c in range(n_dil)]
            outs, lses = [], []
            for g in range(n_dil):
                o, lse = _dil_attn(qs[g], kv_sh[g], kv_sh[n_dil + g])
                outs.append(o)
                lses.append(lse)
            h = _dil_combine(outs, lses, dil_w_o16, j, h, seq=seq, tm=256)
        h = _mem_attn(h, norm_mem[i], mem_w_q16, mem_kv, mem_w_o16, i, batch=batch, seq=seq, tm=1024)
        h = _ffn(h, norm_ffn[i], ffn_w_in16, ffn_w_out16, i, norm_final, tm=1024, tf=512,
                 final_norm=(i == depth - 1))
    return h.reshape(batch, seq, d_model)
```

```python
import functools
import math

import jax
import jax.numpy as jnp
import numpy as np
from jax import lax
from jax.experimental import pallas as pl
from jax.experimental.pallas import tpu as pltpu

F32 = jnp.float32
BF16 = jnp.bfloat16

NORM_EPS = 1e-6
SSD_HEAD_DIM = 64
SSD_GROUPS = 8
SSD_STATE = 128
SSD_CONV = 4
SSD_CHUNK = 128
DIL_PATTERNS = ((128, 1), (512, 4), (2048, 16))
DIL_HEADS = 16
DIL_HEAD_DIM = 128
DIL_BLOCK = 128
ROPE_THETA = 10000.0
LOG2E = math.log2(math.e)
MEM_HEADS = 4
MEM_HEAD_DIM = 128

LANES = 128
SUBLANES = 8
VMEM_LIMIT = 56 * 1024 * 1024
ROW_SUB = 256
WIDE_ROW_SUB = 512
ATTN_BLOCKS_PER_STEP = 8


def _cparams(*sem):
    return pltpu.CompilerParams(dimension_semantics=sem, vmem_limit_bytes=VMEM_LIMIT)


def _rms(x, g):
    ms = jnp.mean(x * x, axis=-1, keepdims=True)
    return x * lax.rsqrt(ms + NORM_EPS) * g


def _silu(x):
    return x / (1.0 + jnp.exp(-x))


def _rope_table_kernel(pos_ref, freq_ref, cos_ref, sin_ref):
    ang = pos_ref[...].astype(F32) * freq_ref[...]
    lane = lax.broadcasted_iota(jnp.int32, ang.shape, 1)
    s = jnp.sin(ang)
    cos_ref[...] = jnp.cos(ang)
    sin_ref[...] = jnp.where(lane < DIL_HEAD_DIM // 2, -s, s)


def _rope_tables(positions):
    t = positions.size
    half = DIL_HEAD_DIM // 2
    inv_freq = ROPE_THETA ** (-jnp.arange(half, dtype=F32) / half)
    freq = jnp.concatenate([inv_freq, inv_freq]).reshape(1, DIL_HEAD_DIM)
    tm = 1024
    return pl.pallas_call(
        _rope_table_kernel,
        grid=(t // tm,),
        in_specs=[pl.BlockSpec((tm, 1), lambda i: (i, 0)),
                  pl.BlockSpec((1, DIL_HEAD_DIM), lambda i: (0, 0))],
        out_specs=[pl.BlockSpec((tm, DIL_HEAD_DIM), lambda i: (i, 0))] * 2,
        out_shape=[jax.ShapeDtypeStruct((t, DIL_HEAD_DIM), F32)] * 2,
        compiler_params=_cparams("parallel"),
        name="rope_tables",
    )(positions.reshape(t, 1), freq)


def _normed_rows(x_ref, g_ref, xn_ref, rows, first):
    if first:
        xn = _rms(x_ref[rows, :], g_ref[...]).astype(BF16)
        xn_ref[rows, :] = xn
        return xn
    return xn_ref[rows, :]


def _row_subtiles(tm, sub=ROW_SUB):
    return [slice(k * sub, (k + 1) * sub) for k in range(tm // sub)]


def _first_step_variants(step_id, body):
    pl.when(step_id == 0)(functools.partial(body, True))
    pl.when(step_id != 0)(functools.partial(body, False))


def _ssd_in_proj_kernel(x_ref, g_ref, w_ref, wdt_ref, o_ref, dt_ref, xn_ref):
    def body(first):
        for rows in _row_subtiles(x_ref.shape[0], WIDE_ROW_SUB):
            xn = _normed_rows(x_ref, g_ref, xn_ref, rows, first)
            o_ref[rows, :] = jnp.dot(xn, w_ref[...], preferred_element_type=F32)
            if first:
                dt_ref[rows, :] = jnp.dot(xn, wdt_ref[...], preferred_element_type=F32)

    _first_step_variants(pl.program_id(1), body)


def _ssd_in_proj(x, g, w, w_dt, layer, *, n_cols, tm, tn):
    t, d = x.shape
    return pl.pallas_call(
        _ssd_in_proj_kernel,
        grid=(t // tm, n_cols // tn),
        in_specs=[pl.BlockSpec((tm, d), lambda i, j: (i, 0)),
                  pl.BlockSpec((1, d), lambda i, j: (0, 0)),
                  pl.BlockSpec((None, d, tn), lambda i, j: (layer, 0, j)),
                  pl.BlockSpec((None, d, LANES), lambda i, j: (layer, 0, 0))],
        out_specs=[pl.BlockSpec((tm, tn), lambda i, j: (i, j)),
                   pl.BlockSpec((tm, LANES), lambda i, j: (i, 0))],
        out_shape=[jax.ShapeDtypeStruct((t, n_cols), F32), jax.ShapeDtypeStruct((t, LANES), F32)],
        scratch_shapes=[pltpu.VMEM((tm, d), BF16)],
        compiler_params=_cparams("parallel", "arbitrary"),
        name="ssd_in_proj",
    )(x, g.reshape(1, d), w, w_dt)


def _residue_permutation(d):
    per_res = ROW_SUB // d
    out_row = lax.broadcasted_iota(jnp.int32, (ROW_SUB, ROW_SUB), 0)
    in_row = lax.broadcasted_iota(jnp.int32, (ROW_SUB, ROW_SUB), 1)
    src = jnp.bitwise_and(out_row, per_res - 1) * d + jnp.right_shift(out_row, per_res.bit_length() - 1)
    return jnp.where(in_row == src, 1.0, 0.0).astype(BF16)


def _norm_matmul_dil_kernel(x_ref, g_ref, w_ref, cos_ref, sin_ref, o_ref, xn_ref, acc_ref, *, dilation, rope,
                            col_tiles):
    d = dilation
    tn = w_ref.shape[1]
    per_res = ROW_SUB // d
    permute_rows = d > 1 and per_res <= 2 * SUBLANES

    def residue_major(ref, k):
        return jnp.concatenate([ref[pl.ds(k * ROW_SUB + r, per_res, stride=d), :] for r in range(d)], axis=0)

    def body(first):
        perm = _residue_permutation(d) if (first and permute_rows) else None
        for k, rows in enumerate(_row_subtiles(x_ref.shape[0])):
            if first and permute_rows:
                xn = _rms(x_ref[rows, :], g_ref[...]).astype(BF16)
                xn = jnp.dot(perm, xn, preferred_element_type=F32).astype(BF16)
                xn_ref[rows, :] = xn
            else:
                xn = _normed_rows(x_ref, g_ref, xn_ref, rows, first)
            acc = jnp.dot(xn, w_ref[...], preferred_element_type=F32)
            if rope:
                cos = residue_major(cos_ref, k) if permute_rows else cos_ref[rows, :]
                sin = residue_major(sin_ref, k) if permute_rows else sin_ref[rows, :]
            for hh in range(tn // DIL_HEAD_DIM):
                sl = slice(hh * DIL_HEAD_DIM, (hh + 1) * DIL_HEAD_DIM)
                t = acc[:, sl]
                if rope:
                    t = t * cos + pltpu.roll(t, DIL_HEAD_DIM // 2, 1) * sin
                if d == 1:
                    o_ref[0, rows, sl] = t.astype(BF16)
                elif permute_rows:
                    t16 = t.astype(BF16)
                    for r in range(d):
                        o_ref[r, k * per_res:(k + 1) * per_res, sl] = t16[r * per_res:(r + 1) * per_res, :]
                else:
                    acc_ref[k, hh] = t
                    for r in range(d):
                        o_ref[r, k * per_res:(k + 1) * per_res, sl] = (
                            acc_ref[k, hh, pl.ds(r, per_res, stride=d), :].astype(BF16))

    if col_tiles == 1:
        body(True)
    else:
        _first_step_variants(pl.program_id(1), body)


def _norm_matmul_dil(x, g, w, layer, rope_tables, *, col_block, dilation, rope, batch, seq, tm, tn, name):
    t, d_model = x.shape
    width = DIL_HEADS * DIL_HEAD_DIM
    tiles = width // tn
    per_b = seq // tm
    d = dilation
    strided_relayout = d > 1 and ROW_SUB // d > 2 * SUBLANES
    acc_shape = ((tm // ROW_SUB, tn // DIL_HEAD_DIM, ROW_SUB, DIL_HEAD_DIM) if strided_relayout
                 else (1, 1, SUBLANES, DIL_HEAD_DIM))
    return pl.pallas_call(
        functools.partial(_norm_matmul_dil_kernel, dilation=d, rope=rope, col_tiles=tiles),
        grid=(t // tm, tiles),
        in_specs=[pl.BlockSpec((tm, d_model), lambda i, j: (i, 0)),
                  pl.BlockSpec((1, d_model), lambda i, j: (0, 0)),
                  pl.BlockSpec((None, d_model, tn), lambda i, j: (layer, 0, col_block * tiles + j)),
                  pl.BlockSpec((tm, DIL_HEAD_DIM), lambda i, j: (i, 0)),
                  pl.BlockSpec((tm, DIL_HEAD_DIM), lambda i, j: (i, 0))],
        out_specs=pl.BlockSpec((None, d, tm // d, tn), lambda i, j: (i // per_b, 0, i % per_b, j)),
        out_shape=jax.ShapeDtypeStruct((batch, d, seq // d, width), BF16),
        scratch_shapes=[pltpu.VMEM((tm, d_model), BF16), pltpu.VMEM(acc_shape, F32)],
        compiler_params=_cparams("parallel", "arbitrary"),
        name=name,
    )(x, g.reshape(1, d_model), w, *rope_tables)


def _matmul_residual_kernel(a_ref, w_ref, h_ref, o_ref):
    o_ref[...] = h_ref[...] + jnp.dot(a_ref[...], w_ref[...], preferred_element_type=F32)


def _matmul_residual(a, w, layer, h, *, tm, tn, name="matmul_residual"):
    t, k = a.shape
    n = w.shape[2]
    return pl.pallas_call(
        _matmul_residual_kernel,
        grid=(n // tn, t // tm),
        in_specs=[pl.BlockSpec((tm, k), lambda j, i: (i, 0)),
                  pl.BlockSpec((None, k, tn), lambda j, i: (layer, 0, j)),
                  pl.BlockSpec((tm, tn), lambda j, i: (i, j))],
        out_specs=pl.BlockSpec((tm, tn), lambda j, i: (i, j)),
        out_shape=jax.ShapeDtypeStruct((t, n), F32),
        compiler_params=_cparams("parallel", "parallel"),
        name=name,
    )(a, w, h)


def _ssd_kernel(z_ref, xs_ref, b_ref, c_ref, dt_ref, convw_ref, convb_ref, dtb_ref, aneg_ref,
                dexp_ref, nw_ref, o_ref, xbuf, xc, state, *, chunks_per_seq):
    L = SSD_CHUNK
    d_inner = xs_ref.shape[1]
    bc_dim = b_ref.shape[1]
    conv_dim = d_inner + 2 * bc_dim
    gw = d_inner // SSD_GROUPS
    halo = SUBLANES
    first = pl.program_id(0) % chunks_per_seq == 0

    @pl.when(first)
    def _():
        xbuf[:, 0:halo, :] = jnp.zeros((xbuf.shape[0], halo, LANES), F32)
        state[...] = jnp.zeros(state.shape, F32)

    @pl.when(jnp.logical_not(first))
    def _():
        xbuf[:, 0:halo, :] = xbuf[:, L:L + halo, :]

    n_cb = conv_dim // LANES
    for cb in range(n_cb):
        c0 = cb * LANES
        if c0 < d_inner:
            src = xs_ref[:, c0:c0 + LANES]
        elif c0 < d_inner + bc_dim:
            src = b_ref[:, c0 - d_inner:c0 - d_inner + LANES]
        else:
            src = c_ref[:, c0 - d_inner - bc_dim:c0 - d_inner - bc_dim + LANES]
        xbuf[cb, halo:halo + L, :] = src

    nres = SUBLANES
    rows = L // nres
    for cb in range(n_cb):
        sl = slice(cb * LANES, (cb + 1) * LANES)
        taps = [convw_ref[k:k + 1, sl] for k in range(SSD_CONV)]
        bias = convb_ref[:, sl]
        cur = [xbuf[cb, pl.ds(halo + r, rows, stride=nres), :] for r in range(nres)]
        prv = {r: xbuf[cb, pl.ds(r, rows, stride=nres), :]
               for r in range(nres - SSD_CONV + 1, nres)}
        for r in range(nres):
            acc = bias
            for k in range(SSD_CONV):
                q = r - (SSD_CONV - 1) + k
                acc = acc + taps[k] * (cur[q] if q >= 0 else prv[q + nres])
            xc[cb, pl.ds(r, rows, stride=nres), :] = _silu(acc)

    raw = dt_ref[...] + dtb_ref[...]
    dt = jnp.maximum(raw, 0.0) + jnp.log1p(jnp.exp(-jnp.abs(raw)))
    acum = dt * aneg_ref[...]
    row = lax.broadcasted_iota(jnp.int32, (L, LANES), 0)
    sh = 1
    while sh < L:
        acum = acum + jnp.where(row >= sh, pltpu.roll(acum, sh, 0), 0.0)
        sh *= 2
    acum2 = acum * LOG2E
    acum2_t = acum2.T
    dt_t = dt.T
    ea = jnp.exp(acum)
    wend = jnp.exp(acum[L - 1:L, :] - acum) * dt

    ti = lax.broadcasted_iota(jnp.int32, (L, L), 0)
    si = lax.broadcasted_iota(jnp.int32, (L, L), 1)
    causal = ti >= si
    lo_half = si < SSD_HEAD_DIM

    def pair_expand(v, h0):
        return jnp.where(lo_half, v[:, h0:h0 + 1], v[:, h0 + 1:h0 + 2])

    heads_per_group = gw // SSD_HEAD_DIM
    for g in range(SSD_GROUPS):
        bg = xc[(d_inner + g * SSD_STATE) // LANES]
        cg = xc[(d_inner + bc_dim + g * SSD_STATE) // LANES]
        cb16 = cg.astype(BF16)
        cb = lax.dot_general(cb16, bg.astype(BF16), (((1,), (1,)), ((), ())),
                             preferred_element_type=F32)
        bt16 = bg.T.astype(BF16)
        st_old = state[g]
        yoff = jnp.dot(cb16, st_old.astype(BF16), preferred_element_type=F32)
        ys, xws, cds = [], [], []
        for pp in range(heads_per_group // 2):
            h0 = g * heads_per_group + 2 * pp
            c0 = g * gw + pp * LANES
            xpair = xc[c0 // LANES]
            xp16 = xpair.astype(BF16)
            yd = []
            for hh in (h0, h0 + 1):
                seg = acum2[:, hh:hh + 1] - acum2_t[hh:hh + 1, :]
                dec = jnp.exp2(jnp.where(causal, seg, -jnp.inf))
                m = (cb * dec * dt_t[hh:hh + 1, :]).astype(BF16)
                yd.append(jnp.dot(m, xp16, preferred_element_type=F32))
            ea_pair = pair_expand(ea, h0)
            y = (jnp.where(lo_half, yd[0], yd[1])
                 + yoff[:, pp * LANES:(pp + 1) * LANES] * ea_pair
                 + xpair * dexp_ref[:, c0:c0 + LANES])
            y = y * _silu(z_ref[:, c0:c0 + LANES])
            ys.append(y)
            xws.append((xpair * pair_expand(wend, h0)).astype(BF16))
            cds.append(ea_pair[L - 1:L, :])
        xw = jnp.concatenate(xws, axis=1)
        cd = jnp.concatenate(cds, axis=1)
        state[g] = st_old * cd + jnp.dot(bt16, xw, preferred_element_type=F32)
        yg = jnp.concatenate(ys, axis=1)
        ms = jnp.mean(yg * yg, axis=-1, keepdims=True)
        yn = yg * lax.rsqrt(ms + NORM_EPS) * nw_ref[:, g * gw:(g + 1) * gw]
        o_ref[:, g * gw:(g + 1) * gw] = yn.astype(o_ref.dtype)


def _ssd_core(proj, dt_raw, conv_w, conv_b, dt_bias, a_log, d_skip, norm_w, *, seq, d_inner):
    t = proj.shape[0]
    L = SSD_CHUNK
    bc_dim = SSD_GROUPS * SSD_STATE
    conv_dim = d_inner + 2 * bc_dim
    heads = d_inner // SSD_HEAD_DIM
    pad = LANES - heads

    def padrow(v):
        return jnp.pad(v.astype(F32), (0, pad)).reshape(1, LANES)

    aneg = padrow(-jnp.exp(a_log.astype(F32)))
    dexp = jnp.repeat(d_skip.astype(F32), SSD_HEAD_DIM).reshape(1, d_inner)
    xs_blk = d_inner // d_inner
    b_blk = (2 * d_inner) // bc_dim
    c_blk = b_blk + 1
    const = lambda c: (0, 0)
    return pl.pallas_call(
        functools.partial(_ssd_kernel, chunks_per_seq=seq // L),
        grid=(t // L,),
        in_specs=[pl.BlockSpec((L, d_inner), lambda c: (c, 0)),
                  pl.BlockSpec((L, d_inner), lambda c: (c, xs_blk)),
                  pl.BlockSpec((L, bc_dim), lambda c: (c, b_blk)),
                  pl.BlockSpec((L, bc_dim), lambda c: (c, c_blk)),
                  pl.BlockSpec((L, LANES), lambda c: (c, 0)),
                  pl.BlockSpec((SSD_CONV, conv_dim), const),
                  pl.BlockSpec((1, conv_dim), const),
                  pl.BlockSpec((1, LANES), const),
                  pl.BlockSpec((1, LANES), const),
                  pl.BlockSpec((1, d_inner), const),
                  pl.BlockSpec((1, d_inner), const)],
        out_specs=pl.BlockSpec((L, d_inner), lambda c: (c, 0)),
        out_shape=jax.ShapeDtypeStruct((t, d_inner), BF16),
        scratch_shapes=[pltpu.VMEM((conv_dim // LANES, L + 2 * SUBLANES, LANES), F32),
                        pltpu.VMEM((conv_dim // LANES, L, LANES), F32),
                        pltpu.VMEM((SSD_GROUPS, SSD_STATE, d_inner // SSD_GROUPS), F32)],
        compiler_params=_cparams("arbitrary"),
        name="ssd_core",
    )(proj, proj, proj, proj, dt_raw, conv_w.astype(F32), conv_b.reshape(1, conv_dim).astype(F32),
      padrow(dt_bias), aneg, dexp, norm_w.reshape(1, d_inner).astype(F32))


def _dil_attn_kernel(q_ref, k_ref, v_ref, o_ref, st_ref, kprev, vprev, bias_ref):
    blk = DIL_BLOCK
    n_blk = q_ref.shape[0] // blk
    i = pl.program_id(1)

    @pl.when(i == 0)
    def _():
        kprev[...] = jnp.zeros(kprev.shape, kprev.dtype)
        vprev[...] = jnp.zeros(vprev.shape, vprev.dtype)

    qi = lax.broadcasted_iota(jnp.int32, (blk, 2 * blk), 0)
    ki = lax.broadcasted_iota(jnp.int32, (blk, 2 * blk), 1)
    dist = qi + blk - ki
    band = (dist >= 0) & (dist <= blk)
    first_key = jnp.where(i > 0, 0, blk)
    bias_ref[0] = jnp.where(band & (ki >= first_key), 0.0, -jnp.inf)
    bias_ref[1] = jnp.where(band, 0.0, -jnp.inf)
    st_ref[...] = jnp.zeros(st_ref.shape, F32)
    scale = DIL_HEAD_DIM ** -0.5
    for j in range(n_blk):
        rows = slice(j * blk, (j + 1) * blk)
        prev_rows = slice((j - 1) * blk, j * blk)
        for h in range(DIL_HEADS):
            sl = slice(h * DIL_HEAD_DIM, (h + 1) * DIL_HEAD_DIM)
            k_prev = kprev[:, sl] if j == 0 else k_ref[prev_rows, sl]
            v_prev = vprev[:, sl] if j == 0 else v_ref[prev_rows, sl]
            k = jnp.concatenate([k_prev, k_ref[rows, sl]], axis=0)
            v = jnp.concatenate([v_prev, v_ref[rows, sl]], axis=0)
            raw = lax.dot_general(q_ref[rows, sl], k, (((1,), (1,)), ((), ())), preferred_element_type=F32)
            raw = raw + bias_ref[min(j, 1)]
            mx = jnp.max(raw, axis=-1, keepdims=True)
            p = jnp.exp2((raw - mx) * (scale * LOG2E))
            o_ref[rows, sl] = jnp.dot(p.astype(BF16), v, preferred_element_type=F32)
            st_ref[rows, h:h + 1] = mx * scale
            st_ref[rows, DIL_HEADS + h:DIL_HEADS + h + 1] = jnp.sum(p, axis=-1, keepdims=True)
    last = slice((n_blk - 1) * blk, n_blk * blk)
    kprev[...] = k_ref[last, :]
    vprev[...] = v_ref[last, :]


def _dil_attn(q, k, v):
    b, d, n, w = q.shape
    seqs = b * d
    rows = min(n, ATTN_BLOCKS_PER_STEP * DIL_BLOCK)
    blk = pl.BlockSpec((None, rows, w), lambda s, i: (s, i, 0))
    o, st = pl.pallas_call(
        _dil_attn_kernel,
        grid=(seqs, n // rows),
        in_specs=[blk, blk, blk],
        out_specs=[blk, pl.BlockSpec((None, rows, LANES), lambda s, i: (s, i, 0))],
        out_shape=[jax.ShapeDtypeStruct((seqs, n, w), F32),
                   jax.ShapeDtypeStruct((seqs, n, LANES), F32)],
        scratch_shapes=[pltpu.VMEM((DIL_BLOCK, w), BF16), pltpu.VMEM((DIL_BLOCK, w), BF16),
                        pltpu.VMEM((2, DIL_BLOCK, 2 * DIL_BLOCK), F32)],
        compiler_params=_cparams("arbitrary", "arbitrary"),
        name=f"dil_attn_d{d}",
    )(q.reshape(seqs, n, w), k.reshape(seqs, n, w), v.reshape(seqs, n, w))
    return o.reshape(b, d, n, w), st.reshape(b, d, n, LANES)


def _dil_combine_kernel(*refs, dilations):
    n_g = len(dilations)
    o_refs, l_refs = refs[:n_g], refs[n_g:2 * n_g]
    w_ref, h_ref, out_ref, onat, lnat = refs[2 * n_g:]
    tm = h_ref.shape[0]

    def head_slice(h):
        return slice(h * DIL_HEAD_DIM, (h + 1) * DIL_HEAD_DIM)

    o_heads, ls = [], []
    slot = 0
    for gi, d in enumerate(dilations):
        if d == 1:
            o_heads.append(lambda h, rows, gi=gi: o_refs[gi][0, rows, head_slice(h)])
            ls.append(lambda rows, gi=gi: l_refs[gi][0, rows, :])
            continue
        for r in range(d):
            strided = pl.ds(r, tm // d, stride=d)
            lnat[slot, strided, :] = l_refs[gi][r]
            for h in range(DIL_HEADS):
                onat[slot * DIL_HEADS + h, strided, :] = o_refs[gi][r, :, head_slice(h)]
        o_heads.append(lambda h, rows, slot=slot: onat[slot * DIL_HEADS + h, rows, :])
        ls.append(lambda rows, slot=slot: lnat[slot, rows, :])
        slot += 1
    half = tm // 2
    for rows in (slice(0, half), slice(half, tm)):
        stats = [l(rows) for l in ls]
        dens = [pltpu.roll(st, LANES - DIL_HEADS, 1) for st in stats]
        mx = functools.reduce(jnp.maximum, stats)
        es = [jnp.exp(st - mx) for st in stats]
        tot = functools.reduce(lambda u, v: u + v, [dn * e for dn, e in zip(dens, es)])
        ws = [e / tot for e in es]
        cols = []
        for h in range(DIL_HEADS):
            c = ws[0][:, h:h + 1] * o_heads[0](h, rows)
            for gi in range(1, n_g):
                c = c + ws[gi][:, h:h + 1] * o_heads[gi](h, rows)
            cols.append(c.astype(BF16))
        a = jnp.concatenate(cols, axis=1)
        out_ref[rows, :] = h_ref[rows, :] + jnp.dot(a, w_ref[...], preferred_element_type=F32)


def _dil_combine(os_, lses, w_o, layer, h, *, seq, tm):
    t, d_model = h.shape
    w = w_o.shape[1]
    dilations = tuple(o.shape[1] for o in os_)
    per_b = seq // tm
    n_strided = sum(1 for d in dilations if d > 1)

    def res_spec(d, width):
        return pl.BlockSpec((None, d, tm // d, width), lambda i: (i // per_b, 0, i % per_b, 0))

    row = lambda i: (i, 0)
    return pl.pallas_call(
        functools.partial(_dil_combine_kernel, dilations=dilations),
        grid=(t // tm,),
        in_specs=[res_spec(d, w) for d in dilations] + [res_spec(d, LANES) for d in dilations]
                 + [pl.BlockSpec((None, w, d_model), lambda i: (layer, 0, 0)), pl.BlockSpec((tm, d_model), row)],
        out_specs=pl.BlockSpec((tm, d_model), row),
        out_shape=jax.ShapeDtypeStruct((t, d_model), F32),
        scratch_shapes=[pltpu.VMEM((n_strided * DIL_HEADS, tm, DIL_HEAD_DIM), F32),
                        pltpu.VMEM((n_strided, tm, LANES), F32)],
        compiler_params=_cparams("parallel"),
        name="dil_combine",
    )(*os_, *lses, w_o, h)


def _mem_kv_kernel(mem_ref, g_ref, w_ref, o_ref):
    xn = _rms(mem_ref[...], g_ref[...]).astype(BF16)
    o_ref[...] = jnp.dot(xn, w_ref[...], preferred_element_type=F32).astype(o_ref.dtype)


def _mem_kv(mem, g, w_kv):
    b, m, d = mem.shape
    depth, _, n = w_kv.shape
    return pl.pallas_call(
        _mem_kv_kernel,
        grid=(depth, b),
        in_specs=[pl.BlockSpec((None, m, d), lambda l, bb: (bb, 0, 0)),
                  pl.BlockSpec((1, d), lambda l, bb: (0, 0)),
                  pl.BlockSpec((None, d, n), lambda l, bb: (l, 0, 0))],
        out_specs=pl.BlockSpec((None, None, m, n), lambda l, bb: (l, bb, 0, 0)),
        out_shape=jax.ShapeDtypeStruct((depth, b, m, n), BF16),
        compiler_params=_cparams("parallel", "parallel"),
        name="mem_kv",
    )(mem, g.reshape(1, d), w_kv)


def _mem_attn_kernel(h_ref, g_ref, wq_ref, kv_ref, wo_ref, o_ref):
    width = MEM_HEADS * MEM_HEAD_DIM
    scale = MEM_HEAD_DIM ** -0.5
    x = h_ref[...]
    xn = _rms(x, g_ref[...]).astype(BF16)
    q = jnp.dot(xn, wq_ref[...], preferred_element_type=F32)
    outs = []
    for hh in range(MEM_HEADS):
        sl = slice(hh * MEM_HEAD_DIM, (hh + 1) * MEM_HEAD_DIM)
        k = kv_ref[:, sl]
        v = kv_ref[:, width + hh * MEM_HEAD_DIM:width + (hh + 1) * MEM_HEAD_DIM]
        sc = lax.dot_general(q[:, sl].astype(BF16), k, (((1,), (1,)), ((), ())),
                             preferred_element_type=F32) * scale
        mx = jnp.max(sc, axis=-1, keepdims=True)
        p = jnp.exp(sc - mx)
        p = p / jnp.sum(p, axis=-1, keepdims=True)
        outs.append(jnp.dot(p.astype(BF16), v, preferred_element_type=F32).astype(BF16))
    a = jnp.concatenate(outs, axis=1)
    o_ref[...] = x + jnp.dot(a, wo_ref[...], preferred_element_type=F32)


def _mem_attn(h, g, w_q, kv, w_o, layer, *, batch, seq, tm):
    t, d = h.shape
    width = MEM_HEADS * MEM_HEAD_DIM
    m = kv.shape[2]
    per_b = seq // tm
    return pl.pallas_call(
        _mem_attn_kernel,
        grid=(t // tm,),
        in_specs=[pl.BlockSpec((tm, d), lambda i: (i, 0)),
                  pl.BlockSpec((1, d), lambda i: (0, 0)),
                  pl.BlockSpec((None, d, width), lambda i: (layer, 0, 0)),
                  pl.BlockSpec((None, None, m, 2 * width), lambda i: (layer, i // per_b, 0, 0)),
                  pl.BlockSpec((None, width, d), lambda i: (layer, 0, 0))],
        out_specs=pl.BlockSpec((tm, d), lambda i: (i, 0)),
        out_shape=jax.ShapeDtypeStruct((t, d), F32),
        compiler_params=_cparams("parallel"),
        name="mem_attn",
    )(h, g.reshape(1, d), w_q, kv, w_o)


def _ffn_kernel(h_ref, g_ref, wg_ref, wu_ref, wo_ref, gf_ref, o_ref, xn_ref, *, final_norm):
    f = pl.program_id(1)

    def body(first):
        for rows in _row_subtiles(h_ref.shape[0], WIDE_ROW_SUB):
            xn = _normed_rows(h_ref, g_ref, xn_ref, rows, first)
            gate = jnp.dot(xn, wg_ref[...], preferred_element_type=F32)
            up = jnp.dot(xn, wu_ref[...], preferred_element_type=F32)
            a = (_silu(gate) * up).astype(BF16)
            base = h_ref[rows, :] if first else o_ref[rows, :]
            o_ref[rows, :] = base + jnp.dot(a, wo_ref[...], preferred_element_type=F32)

    _first_step_variants(f, body)

    if final_norm:
        @pl.when(f == pl.num_programs(1) - 1)
        def _():
            o_ref[...] = _rms(o_ref[...], gf_ref[...])


def _ffn(h, g, w_in, w_out, layer, g_final, *, tm, tf, final_norm):
    t, d = h.shape
    d_ff = w_out.shape[1]
    nf = d_ff // tf
    return pl.pallas_call(
        functools.partial(_ffn_kernel, final_norm=final_norm),
        grid=(t // tm, nf),
        in_specs=[pl.BlockSpec((tm, d), lambda i, f: (i, 0)),
                  pl.BlockSpec((1, d), lambda i, f: (0, 0)),
                  pl.BlockSpec((None, d, tf), lambda i, f: (layer, 0, f)),
                  pl.BlockSpec((None, d, tf), lambda i, f: (layer, 0, nf + f)),
                  pl.BlockSpec((None, tf, d), lambda i, f: (layer, f, 0)),
                  pl.BlockSpec((1, d), lambda i, f: (0, 0))],
        out_specs=pl.BlockSpec((tm, d), lambda i, f: (i, 0)),
        out_shape=jax.ShapeDtypeStruct((t, d), F32),
        scratch_shapes=[pltpu.VMEM((tm, d), BF16)],
        compiler_params=_cparams("parallel", "arbitrary"),
        name="ffn",
    )(h, g.reshape(1, d), w_in, w_in, w_out, g_final.reshape(1, d))


def kernel(x, mem, positions, norm_mix, norm_mem, norm_ffn, norm_final, ssd_w_in, ssd_conv_w, ssd_conv_b, ssd_dt_bias, ssd_a_log, ssd_d, ssd_norm, ssd_w_out, kv_norm, w_kv_shared, dil_w_q, dil_w_o, mem_src_norm, mem_w_q, mem_w_kv, mem_w_o, ffn_w_in, ffn_w_out):
    batch, seq, d_model = x.shape
    depth = norm_mix.shape[0]
    n_a = ssd_w_in.shape[0]
    d_inner = ssd_w_out.shape[1]
    heads = d_inner // SSD_HEAD_DIM
    zx_dim = ssd_w_in.shape[2] - heads
    dil_w = DIL_HEADS * DIL_HEAD_DIM
    n_dil = len(DIL_PATTERNS)
    t = batch * seq

    h = x.reshape(t, d_model)
    rope = _rope_tables(positions)
    mem_kv = _mem_kv(mem, mem_src_norm, mem_w_kv.astype(BF16))
    mem_w_q16 = mem_w_q.astype(BF16)
    mem_w_o16 = mem_w_o.astype(BF16)
    ffn_w_in16 = ffn_w_in.astype(BF16)
    ffn_w_out16 = ffn_w_out.astype(BF16)
    ssd_w_in16 = ssd_w_in.astype(BF16)
    ssd_w_out16 = ssd_w_out.astype(BF16)
    w_kv16 = w_kv_shared.astype(BF16)[None]
    dil_w_q16 = dil_w_q.astype(BF16)
    dil_w_o16 = dil_w_o.astype(BF16)
    ssd_w_dt16 = jnp.pad(ssd_w_in[:, :, zx_dim:], ((0, 0), (0, 0), (0, LANES - heads))).astype(BF16)

    kv_sh = None
    for i in range(depth):
        if i < n_a:
            proj, dt_raw = _ssd_in_proj(h, norm_mix[i], ssd_w_in16, ssd_w_dt16, i, n_cols=zx_dim,
                                        tm=1024, tn=1024)
            y = _ssd_core(proj, dt_raw, ssd_conv_w[i], ssd_conv_b[i], ssd_dt_bias[i], ssd_a_log[i],
                          ssd_d[i], ssd_norm[i], seq=seq, d_inner=d_inner)
            h = _matmul_residual(y, ssd_w_out16, i, h, tm=512, tn=1024, name="ssd_out_proj")
        else:
            dils = tuple(d for _, d in DIL_PATTERNS)
            assert all(w // d == DIL_BLOCK for w, d in DIL_PATTERNS)
            proj = functools.partial(_norm_matmul_dil, batch=batch, seq=seq, tm=1024, tn=dil_w)
            if kv_sh is None:
                kv_sh = [proj(h, kv_norm, w_kv16, 0, rope, col_block=c, dilation=dils[c % n_dil],
                              rope=c < n_dil, name=f"shared_kv_proj_{c}") for c in range(2 * n_dil)]
            j = i - n_a
            qs = [proj(h, norm_mix[i], dil_w_q16, j, rope, col_block=c, dilation=dils[c], rope=True,
                       name=f"dil_q_proj_{c}") for c in range(n_dil)]
            outs, lses = [], []
            for g in range(n_dil):
                o, lse = _dil_attn(qs[g], kv_sh[g], kv_sh[n_dil + g])
                outs.append(o)
                lses.append(lse)
            h = _dil_combine(outs, lses, dil_w_o16, j, h, seq=seq, tm=256)
        h = _mem_attn(h, norm_mem[i], mem_w_q16, mem_kv, mem_w_o16, i, batch=batch, seq=seq, tm=1024)
        h = _ffn(h, norm_ffn[i], ffn_w_in16, ffn_w_out16, i, norm_final, tm=1024, tf=512,
                 final_norm=(i == depth - 1))
    return h.reshape(batch, seq, d_model)
```
